```python
import jax, jax.numpy as jnp
from jax import lax
import numpy as np

D_MODEL = 1024
BATCH = 1
SEQ = 16384
DEPTH = 1
DEC_BATCH = 16
DEC_SEQ = 16
PAST_LEN = 2048

CHUNK = 64
GLA_HEADS = 4
GLA_DK = 128
GLA_DV = 256
GLA_KEY = GLA_HEADS * GLA_DK
GLA_VAL = GLA_HEADS * GLA_DV
GLA_LOWRANK = 16
GLA_TAU = 16.0
GMLP_CHUNK = 128
GMLP_GROUPS = 4
GMLP_WIDTH = 1024
GMLP_DG = GMLP_WIDTH // GMLP_GROUPS
N_BRANCH = 2
EPS = 1e-6
PROJ_SIZES = (GLA_KEY, GLA_KEY, GLA_VAL, GLA_VAL, GLA_LOWRANK,
              GMLP_WIDTH, GMLP_WIDTH, GMLP_WIDTH, D_MODEL, D_MODEL)
PROJ_COLS = GLA_KEY * 2 + GLA_VAL * 2 + GLA_LOWRANK + GMLP_WIDTH * 3 + D_MODEL * N_BRANCH

kernel_name = "hybrid_gla_gmlp_stream_step"


def _split_points():
    pts, acc = [], 0
    for s in PROJ_SIZES[:-1]:
        acc += s
        pts.append(acc)
    return pts


def rmsnorm(x, w):
    xf = x.astype(jnp.float32)
    y = xf * lax.rsqrt(jnp.mean(xf * xf, axis=-1, keepdims=True) + EPS)
    return (y * w.astype(jnp.float32)).astype(x.dtype)


def layernorm(x, w, b):
    xf = x.astype(jnp.float32)
    mu = jnp.mean(xf, axis=-1, keepdims=True)
    var = jnp.mean(jnp.square(xf - mu), axis=-1, keepdims=True)
    y = (xf - mu) * lax.rsqrt(var + EPS)
    return (y * w.astype(jnp.float32) + b.astype(jnp.float32)).astype(x.dtype)


def gla_chunked(q, k, v, log_a, S0):
    B, L, H, _ = q.shape
    C = min(CHUNK, L)
    N = L // C
    def to_chunks(t):
        return t.astype(jnp.float32).reshape(B, N, C, H, t.shape[-1]).transpose(1, 0, 2, 3, 4)
    qc, kc, vc = to_chunks(q), to_chunks(k), to_chunks(v)
    bc = jnp.cumsum(to_chunks(log_a), axis=2)
    mask = jnp.tril(jnp.ones((C, C), dtype=bool))

    def step(S, inp):
        qn, kn, vn, bn = inp
        diff = bn[:, :, None] - bn[:, None, :]
        dec = jnp.exp(jnp.where(mask[None, :, :, None, None], diff, -jnp.inf))
        A = jnp.einsum('bthk,btshk,bshk->bhts', qn, dec, kn)
        intra = jnp.einsum('bhts,bshv->bthv', A, vn)
        inter = jnp.einsum('bthk,bhkv->bthv', qn * jnp.exp(bn), S)
        b_last = bn[:, -1]
        S = jnp.exp(b_last)[..., None] * S + jnp.einsum(
            'bshk,bshv->bhkv', kn * jnp.exp(b_last[:, None] - bn), vn)
        return S, intra + inter

    S_fin, o = lax.scan(step, S0.astype(jnp.float32), (qc, kc, vc, bc))
    o = o.transpose(1, 0, 2, 3, 4).reshape(B, L, H, v.shape[-1])
    return o, S_fin


def mixer_layer(x, c, S0, norm_w, w_ada, b_ada, w_in, w_a2, b_a, gla_norm_w,
                ln_v_w, ln_v_b, w_s, b_s, b_gate, w_proj_a, w_proj_b, w_out):
    B, L, _ = x.shape
    ada = jax.nn.silu(c) @ w_ada + b_ada
    shift, scale, gate = jnp.split(ada[:, None, :], 3, axis=-1)
    h = rmsnorm(x, norm_w) * (1.0 + scale) + shift
    proj = h @ w_in
    q, k, v, r, a_lr, u, gv, z, g_a, g_b = jnp.split(proj, _split_points(), axis=-1)

    log_a = jax.nn.log_sigmoid((a_lr @ w_a2 + b_a).astype(jnp.float32)) / GLA_TAU
    q = q.reshape(B, L, GLA_HEADS, GLA_DK) * (GLA_DK ** -0.5)
    k = k.reshape(B, L, GLA_HEADS, GLA_DK)
    v = v.reshape(B, L, GLA_HEADS, GLA_DV)
    log_a = log_a.reshape(B, L, GLA_HEADS, GLA_DK)
    o, S_new = gla_chunked(q, k, v, log_a, S0)
    o = rmsnorm(o, gla_norm_w).astype(x.dtype).reshape(B, L, GLA_VAL)
    y_a = (o * jax.nn.silu(r)) @ w_proj_a

    vn = layernorm(gv, ln_v_w, ln_v_b)
    C = min(GMLP_CHUNK, L)
    N = L // C
    idx = jnp.arange(C) // CHUNK
    smask = idx[:, None] >= idx[None, :]
    ws = jnp.where(smask[None], w_s[:, :C, :C], 0.0)
    vg = vn.reshape(B, N, C, GMLP_GROUPS, GMLP_DG)
    s = jnp.einsum('gij,bnjgc->bnigc', ws, vg) + b_s[:, :C].T[None, None, :, :, None]
    s = s.reshape(B, L, GMLP_WIDTH)
    y_b = (u * s * jax.nn.silu(z)) @ w_proj_b

    merged = jax.nn.sigmoid(g_a + b_gate[0]) * y_a + jax.nn.sigmoid(g_b + b_gate[1]) * y_b
    x = x + gate * (merged @ w_out)
    return x, S_new, vn


def setup_inputs(seed: int = 0) -> dict:
    key = jax.random.key(seed)
    ks = jax.random.split(key, 24)
    f32 = jnp.float32
    nrm = lambda k, shape, s: jax.random.normal(k, shape, f32) * s
    return {
        'x_prompt': nrm(ks[0], (BATCH, SEQ, D_MODEL), 1.0),
        'x_sample': nrm(ks[1], (DEC_BATCH, DEC_SEQ, D_MODEL), 1.0),
        'state_gla': nrm(ks[2], (DEPTH, DEC_BATCH, GLA_HEADS, GLA_DK, GLA_DV), 0.1),
        'c_prompt': nrm(ks[3], (BATCH, D_MODEL), 1.0),
        'c_sample': nrm(ks[4], (DEC_BATCH, D_MODEL), 1.0),
        'norm_w': 1.0 + nrm(ks[5], (DEPTH, D_MODEL), 0.02),
        'w_ada': nrm(ks[6], (DEPTH, D_MODEL, 3 * D_MODEL), 0.5 * D_MODEL ** -0.5),
        'b_ada': nrm(ks[7], (DEPTH, 3 * D_MODEL), 0.02),
        'w_in': nrm(ks[8], (DEPTH, D_MODEL, PROJ_COLS), D_MODEL ** -0.5),
        'w_a2': nrm(ks[9], (DEPTH, GLA_LOWRANK, GLA_KEY), GLA_LOWRANK ** -0.5),
        'b_a': nrm(ks[10], (DEPTH, GLA_KEY), 0.1),
        'gla_norm_w': 1.0 + nrm(ks[11], (DEPTH, GLA_DV), 0.02),
        'ln_v_w': 1.0 + nrm(ks[12], (DEPTH, GMLP_WIDTH), 0.02),
        'ln_v_b': nrm(ks[13], (DEPTH, GMLP_WIDTH), 0.02),
        'w_s': nrm(ks[14], (DEPTH, GMLP_GROUPS, GMLP_CHUNK, GMLP_CHUNK), GMLP_CHUNK ** -0.5),
        'b_s': 1.0 + nrm(ks[15], (DEPTH, GMLP_GROUPS, GMLP_CHUNK), 0.02),
        'b_gate': nrm(ks[16], (DEPTH, N_BRANCH, D_MODEL), 0.02),
        'w_proj_a': nrm(ks[17], (DEPTH, GLA_VAL, D_MODEL), GLA_VAL ** -0.5),
        'w_proj_b': nrm(ks[18], (DEPTH, GMLP_WIDTH, D_MODEL), GMLP_WIDTH ** -0.5),
        'w_out': nrm(ks[19], (DEPTH, D_MODEL, D_MODEL), D_MODEL ** -0.5),
        'final_norm_w': 1.0 + nrm(ks[20], (D_MODEL,), 0.02),
    }


def reference(x_prompt, x_sample, state_gla, c_prompt, c_sample, norm_w, w_ada, b_ada,
              w_in, w_a2, b_a, gla_norm_w, ln_v_w, ln_v_b, w_s, b_s, b_gate,
              w_proj_a, w_proj_b, w_out, final_norm_w):
    xp, xs = x_prompt, x_sample
    sp_list, ss_list, vs_list = [], [], []
    for l in range(DEPTH):
        params = (norm_w[l], w_ada[l], b_ada[l], w_in[l], w_a2[l], b_a[l], gla_norm_w[l],
                  ln_v_w[l], ln_v_b[l], w_s[l], b_s[l], b_gate[l],
                  w_proj_a[l], w_proj_b[l], w_out[l])
        s0_prompt = jnp.zeros((xp.shape[0], GLA_HEADS, GLA_DK, GLA_DV), jnp.float32)
        xp, sp, _ = mixer_layer(xp, c_prompt, s0_prompt, *params)
        xs, ss, vs = mixer_layer(xs, c_sample, state_gla[l], *params)
        sp_list.append(sp)
        ss_list.append(ss)
        vs_list.append(vs)
    y_prompt = rmsnorm(xp, final_norm_w)
    y_sample = rmsnorm(xs, final_norm_w)
    new_state_gla_prompt = jnp.stack(sp_list)
    new_state_gla_sample = jnp.stack(ss_list)
    new_gmlp_v_sample = jnp.stack(vs_list)
    return (y_prompt, y_sample, new_state_gla_prompt, new_state_gla_sample, new_gmlp_v_sample)
```

```python
import functools

import jax
import jax.numpy as jnp
from jax import lax
from jax.experimental import pallas as pl
from jax.experimental.pallas import tpu as pltpu

D_MODEL = 1024
GLA_HEADS = 4
GLA_DK = 128
GLA_DV = 256
GLA_KEY = GLA_HEADS * GLA_DK
GLA_VAL = GLA_HEADS * GLA_DV
GLA_LOWRANK = 16
GLA_TAU = 16.0
GLA_CHUNK = 64
GMLP_CHUNK = 128
GMLP_GROUPS = 4
GMLP_WIDTH = 1024
GMLP_DG = GMLP_WIDTH // GMLP_GROUPS
EPS = 1e-6

LANES = 128
PROMPT_TILE = 256
ADA_COLS = 512
VMEM_LIMIT_BYTES = 56 * 1024 * 1024

_OFF_Q = 0
_OFF_K = _OFF_Q + GLA_KEY
_OFF_V = _OFF_K + GLA_KEY
_OFF_R = _OFF_V + GLA_VAL
_OFF_U = _OFF_R + GLA_VAL
_OFF_GV = _OFF_U + GMLP_WIDTH
_OFF_Z = _OFF_GV + GMLP_WIDTH
_OFF_GA = _OFF_Z + GMLP_WIDTH
_OFF_GB = _OFF_GA + D_MODEL
MAIN_COLS = _OFF_GB + D_MODEL

BF16 = jnp.bfloat16
F32 = jnp.float32


def _mm(a, b):
    return jnp.dot(a.astype(BF16), b.astype(BF16), preferred_element_type=F32)


def _mm_nt(a, b):
    return lax.dot_general(a.astype(BF16), b.astype(BF16), (((1,), (1,)), ((), ())),
                           preferred_element_type=F32)


def _mm_tn(a, b):
    return lax.dot_general(a.astype(BF16), b.astype(BF16), (((0,), (0,)), ((), ())),
                           preferred_element_type=F32)


def _silu(x):
    return x * jax.nn.sigmoid(x)


def _log_sigmoid(x):
    return jnp.minimum(x, 0.0) - jnp.log1p(jnp.exp(-jnp.abs(x)))


def _rms(x, w):
    return x * lax.rsqrt(jnp.mean(x * x, axis=-1, keepdims=True) + EPS) * w


def _tril_ones(n):
    r = lax.broadcasted_iota(jnp.int32, (n, n), 0)
    c = lax.broadcasted_iota(jnp.int32, (n, n), 1)
    return r >= c


def _cumsum_rows(la, tril_bf16):
    hi = la.astype(BF16)
    lo = (la - hi.astype(F32)).astype(BF16)
    return (jnp.dot(tril_bf16, hi, preferred_element_type=F32)
            + jnp.dot(tril_bf16, lo, preferred_element_type=F32))


def _gla_chunk(q, k, v, la, states, tril_mask, tril_bf16, transposed_state):
    c = q.shape[0]
    b = _cumsum_rows(la, tril_bf16)
    mid = b[c // 2:c // 2 + 1, :]
    last = b[c - 1:c, :]
    qd = q * (GLA_DK ** -0.5) * jnp.exp(b - mid)
    kd = k * jnp.exp(mid - b)
    qs = qd * jnp.exp(mid)
    kl = kd * jnp.exp(last - mid)
    e_last = jnp.exp(last)
    outs, new_states = [], []
    for h in range(GLA_HEADS):
        ks = slice(h * GLA_DK, (h + 1) * GLA_DK)
        vs = slice(h * GLA_DV, (h + 1) * GLA_DV)
        a = jnp.where(tril_mask, _mm_nt(qd[:, ks], kd[:, ks]), 0.0)
        v_h = v[:, vs]
        s_h = states[h]
        if transposed_state:
            inter = _mm_nt(qs[:, ks], s_h)
            s_new = s_h * e_last[:, ks] + _mm_tn(v_h, kl[:, ks])
        else:
            inter = _mm(qs[:, ks], s_h)
            dec = jnp.broadcast_to(e_last[:, ks], (GLA_DK, GLA_DK)).T
            dec = jnp.concatenate([dec] * (GLA_DV // GLA_DK), axis=1)
            s_new = s_h * dec + _mm_tn(kl[:, ks], v_h)
        outs.append(_mm(a, v_h) + inter)
        new_states.append(s_new)
    return outs, new_states


def _head_rms_gate(o_heads, r, gla_norm_w):
    cols = []
    for h in range(GLA_HEADS):
        vs = slice(h * GLA_DV, (h + 1) * GLA_DV)
        cols.append(_rms(o_heads[h], gla_norm_w) * _silu(r[:, vs]))
    return jnp.concatenate(cols, axis=1)


def _layernorm(x, w, b):
    mu = jnp.mean(x, axis=-1, keepdims=True)
    xc = x - mu
    var = jnp.mean(xc * xc, axis=-1, keepdims=True)
    return xc * lax.rsqrt(var + EPS) * w + b


def _spatial_gate(vn, ws_bf16, bs_cols):
    cols = []
    for g in range(GMLP_GROUPS):
        gs = slice(g * GMLP_DG, (g + 1) * GMLP_DG)
        cols.append(_mm(ws_bf16[g], vn[:, gs]) + bs_cols[:, g:g + 1])
    return jnp.concatenate(cols, axis=1)


def _masked_ws(ws, c):
    r = lax.broadcasted_iota(jnp.int32, (c, c), 0) // GLA_CHUNK
    s = lax.broadcasted_iota(jnp.int32, (c, c), 1) // GLA_CHUNK
    return jnp.where((r >= s)[None], ws, 0.0)


def _ada_kernel(c_ref, w_ref, b_ref, o_ref):
    o_ref[...] = _mm(_silu(c_ref[...]), w_ref[...]) + b_ref[...]


def _ada_call(c_all, w_ada, b_ada):
    rows = c_all.shape[0]
    return pl.pallas_call(
        _ada_kernel,
        grid=(3 * D_MODEL // ADA_COLS,),
        in_specs=[pl.BlockSpec((rows, D_MODEL), lambda j: (0, 0)),
                  pl.BlockSpec((D_MODEL, ADA_COLS), lambda j: (0, j)),
                  pl.BlockSpec((1, ADA_COLS), lambda j: (0, j))],
        out_specs=pl.BlockSpec((rows, ADA_COLS), lambda j: (0, j)),
        out_shape=jax.ShapeDtypeStruct((rows, 3 * D_MODEL), F32),
        name="ada_proj",
    )(c_all, w_ada, b_ada)


def _modulated_input(x, shift, scale, norm_w):
    return (_rms(x, norm_w) * (1.0 + scale) + shift).astype(BF16)


def _proj(h, w_ref, off, width):
    return jnp.dot(h, w_ref[:, off:off + width], preferred_element_type=F32)


def _log_decay(h, w_alr_ref, w_a2_ref, b_a):
    a_lr = jnp.dot(h, w_alr_ref[...], preferred_element_type=F32)
    return _log_sigmoid(_mm(a_lr, w_a2_ref[...]) + b_a) * (1.0 / GLA_TAU)


def _merge_out(x, gate, y_a, y_b, g_a, g_b, b_gate, w_out_ref, final_w):
    merged = (jax.nn.sigmoid(g_a + b_gate[0:1, :]) * y_a
              + jax.nn.sigmoid(g_b + b_gate[1:2, :]) * y_b)
    out = x + gate * _mm(merged, w_out_ref[...])
    return _rms(out, final_w)


def _prompt_kernel(x_ref, ada_ref, norm_w_ref, b_a_ref, gla_nw_ref, lnw_ref, lnb_ref, ws_ref,
                   bs_ref, bgate_ref, final_w_ref, w_main_ref, w_alr_ref, w_a2_ref, w_pa_ref,
                   w_pb_ref, w_out_ref, y_ref, state_ref, st_scr, ws_scr):
    i = pl.program_id(0)
    tile = x_ref.shape[0]

    @pl.when(i == 0)
    def _init():
        st_scr[...] = jnp.zeros_like(st_scr)
        ws_scr[...] = _masked_ws(ws_ref[...], GMLP_CHUNK).astype(BF16)

    x = x_ref[...]
    shift = ada_ref[:, 0:D_MODEL]
    scale = ada_ref[:, D_MODEL:2 * D_MODEL]
    gate = ada_ref[:, 2 * D_MODEL:3 * D_MODEL]
    h = _modulated_input(x, shift, scale, norm_w_ref[...])

    la = _log_decay(h, w_alr_ref, w_a2_ref, b_a_ref[...])
    q = _proj(h, w_main_ref, _OFF_Q, GLA_KEY)
    k = _proj(h, w_main_ref, _OFF_K, GLA_KEY)
    v = _proj(h, w_main_ref, _OFF_V, GLA_VAL)
    r = _proj(h, w_main_ref, _OFF_R, GLA_VAL)
    tril_mask = _tril_ones(GLA_CHUNK)
    tril_bf16 = tril_mask.astype(BF16)
    states = [st_scr[hd] for hd in range(GLA_HEADS)]
    gated = []
    for c in range(tile // GLA_CHUNK):
        rows = slice(c * GLA_CHUNK, (c + 1) * GLA_CHUNK)
        o_heads, states = _gla_chunk(q[rows], k[rows], v[rows], la[rows], states,
                                     tril_mask, tril_bf16, transposed_state=True)
        gated.append(_head_rms_gate(o_heads, r[rows], gla_nw_ref[...]))
    for hd in range(GLA_HEADS):
        st_scr[hd] = states[hd]
    y_a = _mm(jnp.concatenate(gated, axis=0), w_pa_ref[...])

    vn = _layernorm(_proj(h, w_main_ref, _OFF_GV, GMLP_WIDTH), lnw_ref[...], lnb_ref[...])
    ws = ws_scr[...]
    s = jnp.concatenate(
        [_spatial_gate(vn[j * GMLP_CHUNK:(j + 1) * GMLP_CHUNK], ws, bs_ref[...])
         for j in range(tile // GMLP_CHUNK)], axis=0)
    u = _proj(h, w_main_ref, _OFF_U, GMLP_WIDTH)
    z = _proj(h, w_main_ref, _OFF_Z, GMLP_WIDTH)
    y_b = _mm(u * s * _silu(z), w_pb_ref[...])

    g_a = _proj(h, w_main_ref, _OFF_GA, D_MODEL)
    g_b = _proj(h, w_main_ref, _OFF_GB, D_MODEL)
    y_ref[...] = _merge_out(x, gate, y_a, y_b, g_a, g_b, bgate_ref[...], w_out_ref,
                            final_w_ref[...])

    @pl.when(i == pl.num_programs(0) - 1)
    def _emit_state():
        for hd in range(GLA_HEADS):
            state_ref[hd] = states[hd].T


def _const_spec(shape):
    nd = len(shape)
    return pl.BlockSpec(shape, lambda i: (0,) * nd, pipeline_mode=pl.Buffered(1))


def _prompt_call(x, ada, small, weights):
    seq = x.shape[0]
    tile = PROMPT_TILE
    in_specs = ([pl.BlockSpec((tile, D_MODEL), lambda i: (i, 0)), _const_spec(ada.shape)]
                + [_const_spec(a.shape) for a in small]
                + [_const_spec(w.shape) for w in weights])
    return pl.pallas_call(
        _prompt_kernel,
        grid=(seq // tile,),
        in_specs=in_specs,
        out_specs=[pl.BlockSpec((tile, D_MODEL), lambda i: (i, 0)),
                   pl.BlockSpec((GLA_HEADS, GLA_DK, GLA_DV), lambda i: (0, 0, 0))],
        out_shape=[jax.ShapeDtypeStruct((seq, D_MODEL), F32),
                   jax.ShapeDtypeStruct((GLA_HEADS, GLA_DK, GLA_DV), F32)],
        scratch_shapes=[pltpu.VMEM((GLA_HEADS, GLA_DV, GLA_DK), F32),
                        pltpu.VMEM((GMLP_GROUPS, GMLP_CHUNK, GMLP_CHUNK), BF16)],
        compiler_params=pltpu.CompilerParams(dimension_semantics=("arbitrary",),
                                             vmem_limit_bytes=VMEM_LIMIT_BYTES),
        name="prompt_layer",
    )(x, ada, *small, *weights)


def _sample_kernel(x_ref, ada_ref, s0_ref, norm_w_ref, b_a_ref, gla_nw_ref, lnw_ref, lnb_ref,
                   ws_ref, bs_ref, bgate_ref, final_w_ref, w_main_ref, w_alr_ref, w_a2_ref,
                   w_pa_ref, w_pb_ref, w_out_ref, y_ref, state_ref, vn_ref,
                   q_scr, k_scr, v_scr, r_scr, la_scr, gated_scr, s_scr):
    nb, ns, _ = x_ref.shape
    rows = nb * ns
    x3 = x_ref[...]
    shift = ada_ref[:, :, 0:D_MODEL]
    scale = ada_ref[:, :, D_MODEL:2 * D_MODEL]
    gate = ada_ref[:, :, 2 * D_MODEL:3 * D_MODEL]
    h = _modulated_input(x3, shift, scale, norm_w_ref[...]).reshape(rows, D_MODEL)

    la_scr[...] = _log_decay(h, w_alr_ref, w_a2_ref, b_a_ref[...])
    q_scr[...] = _proj(h, w_main_ref, _OFF_Q, GLA_KEY)
    k_scr[...] = _proj(h, w_main_ref, _OFF_K, GLA_KEY)
    v_scr[...] = _proj(h, w_main_ref, _OFF_V, GLA_VAL)
    r_scr[...] = _proj(h, w_main_ref, _OFF_R, GLA_VAL)
    vn = _layernorm(_proj(h, w_main_ref, _OFF_GV, GMLP_WIDTH), lnw_ref[...], lnb_ref[...])
    vn_ref[...] = vn.reshape(nb, ns, GMLP_WIDTH)

    tril_mask = _tril_ones(ns)
    tril_bf16 = tril_mask.astype(BF16)
    ws = _masked_ws(ws_ref[...], ns).astype(BF16)
    bs_cols = bs_ref[...]
    gla_nw = gla_nw_ref[...]

    def stream(b, carry):
        rs = pl.ds(pl.multiple_of(b * ns, ns), ns)
        states = [s0_ref[b, hd] for hd in range(GLA_HEADS)]
        o_heads, states = _gla_chunk(q_scr[rs, :], k_scr[rs, :], v_scr[rs, :], la_scr[rs, :],
                                     states, tril_mask, tril_bf16, transposed_state=False)
        for hd in range(GLA_HEADS):
            state_ref[b, hd] = states[hd]
        gated_scr[rs, :] = _head_rms_gate(o_heads, r_scr[rs, :], gla_nw)
        s_scr[rs, :] = _spatial_gate(vn_ref[b], ws, bs_cols)
        return carry

    lax.fori_loop(0, nb, stream, 0)

    y_a = _mm(gated_scr[...], w_pa_ref[...])
    u = _proj(h, w_main_ref, _OFF_U, GMLP_WIDTH)
    z = _proj(h, w_main_ref, _OFF_Z, GMLP_WIDTH)
    y_b = _mm(u * s_scr[...] * _silu(z), w_pb_ref[...])
    g_a = _proj(h, w_main_ref, _OFF_GA, D_MODEL)
    g_b = _proj(h, w_main_ref, _OFF_GB, D_MODEL)
    merged = (jax.nn.sigmoid(g_a + bgate_ref[0:1, :]) * y_a
              + jax.nn.sigmoid(g_b + bgate_ref[1:2, :]) * y_b)
    upd = _mm(merged, w_out_ref[...]).reshape(nb, ns, D_MODEL)
    y_ref[...] = _rms(x3 + gate * upd, final_w_ref[...])


def _sample_call(x, ada, s0, small, weights):
    nb, ns, _ = x.shape
    rows = nb * ns
    args = (x, ada, s0, *small, *weights)
    full = lambda a: pl.BlockSpec(a.shape, lambda i, nd=a.ndim: (0,) * nd)
    return pl.pallas_call(
        _sample_kernel,
        grid=(1,),
        in_specs=[full(a) for a in args],
        out_specs=[pl.BlockSpec((nb, ns, D_MODEL), lambda i: (0, 0, 0)),
                   pl.BlockSpec(s0.shape, lambda i: (0, 0, 0, 0)),
                   pl.BlockSpec((nb, ns, GMLP_WIDTH), lambda i: (0, 0, 0))],
        out_shape=[jax.ShapeDtypeStruct((nb, ns, D_MODEL), F32),
                   jax.ShapeDtypeStruct(s0.shape, F32),
                   jax.ShapeDtypeStruct((nb, ns, GMLP_WIDTH), F32)],
        scratch_shapes=[pltpu.VMEM((rows, GLA_KEY), F32), pltpu.VMEM((rows, GLA_KEY), F32),
                        pltpu.VMEM((rows, GLA_VAL), F32), pltpu.VMEM((rows, GLA_VAL), F32),
                        pltpu.VMEM((rows, GLA_KEY), F32), pltpu.VMEM((rows, GLA_VAL), F32),
                        pltpu.VMEM((rows, GMLP_WIDTH), F32)],
        compiler_params=pltpu.CompilerParams(dimension_semantics=("arbitrary",),
                                             vmem_limit_bytes=VMEM_LIMIT_BYTES),
        name="sample_layer",
    )(*args)


def kernel(x_prompt, x_sample, state_gla, c_prompt, c_sample, norm_w, w_ada, b_ada, w_in, w_a2,
           b_a, gla_norm_w, ln_v_w, ln_v_b, w_s, b_s, b_gate, w_proj_a, w_proj_b, w_out,
           final_norm_w):
    depth = norm_w.shape[0]
    assert depth == 1 and x_prompt.shape[0] == 1
    nb, ns, _ = x_sample.shape
    lyr = 0

    c_all = jnp.concatenate([c_sample, c_prompt], axis=0)
    pad = (-c_all.shape[0]) % 8
    c_all = jnp.pad(c_all, ((0, pad), (0, 0)))
    ada = _ada_call(c_all, w_ada[lyr], b_ada[lyr][None, :])
    ada_sample = ada[:nb][:, None, :]
    ada_prompt = ada[nb:nb + 1]

    a0 = 2 * GLA_KEY + 2 * GLA_VAL
    w = w_in[lyr]
    w_main = jnp.concatenate([w[:, :a0], w[:, a0 + GLA_LOWRANK:]], axis=1).astype(BF16)
    w_alr = jnp.pad(w[:, a0:a0 + GLA_LOWRANK], ((0, 0), (0, LANES - GLA_LOWRANK))).astype(BF16)
    w_a2p = jnp.pad(w_a2[lyr], ((0, LANES - GLA_LOWRANK), (0, 0))).astype(BF16)
    weights = (w_main, w_alr, w_a2p, w_proj_a[lyr].astype(BF16), w_proj_b[lyr].astype(BF16),
               w_out[lyr].astype(BF16))

    def small(chunk):
        return (norm_w[lyr][None, :], b_a[lyr][None, :], gla_norm_w[lyr][None, :],
                ln_v_w[lyr][None, :], ln_v_b[lyr][None, :], w_s[lyr][:, :chunk, :chunk],
                b_s[lyr][:, :chunk].T, b_gate[lyr], final_norm_w[None, :])

    y_p, st_p = _prompt_call(x_prompt[0], ada_prompt, small(GMLP_CHUNK), weights)
    y_s, st_s, vn_s = _sample_call(x_sample, ada_sample, state_gla[lyr], small(min(GMLP_CHUNK, ns)),
                                   weights)
    return (y_p[None], y_s, st_p[None, None], st_s[None], vn_s[None])
```

```python
import functools

import jax
import jax.numpy as jnp
from jax import lax
from jax.experimental import pallas as pl
from jax.experimental.pallas import tpu as pltpu

D_MODEL = 1024
GLA_HEADS = 4
GLA_DK = 128
GLA_DV = 256
GLA_KEY = GLA_HEADS * GLA_DK
GLA_VAL = GLA_HEADS * GLA_DV
GLA_LOWRANK = 16
GLA_TAU = 16.0
GLA_CHUNK = 64
GMLP_CHUNK = 128
GMLP_GROUPS = 4
GMLP_WIDTH = 1024
GMLP_DG = GMLP_WIDTH // GMLP_GROUPS
EPS = 1e-6

LANES = 128
PROMPT_TILE = 256
ADA_COLS = 512
VMEM_LIMIT_BYTES = 56 * 1024 * 1024

_OFF_Q = 0
_OFF_K = _OFF_Q + GLA_KEY
_OFF_V = _OFF_K + GLA_KEY
_OFF_R = _OFF_V + GLA_VAL
_OFF_U = _OFF_R + GLA_VAL
_OFF_GV = _OFF_U + GMLP_WIDTH
_OFF_Z = _OFF_GV + GMLP_WIDTH
_OFF_GA = _OFF_Z + GMLP_WIDTH
_OFF_GB = _OFF_GA + D_MODEL
MAIN_COLS = _OFF_GB + D_MODEL

BF16 = jnp.bfloat16
F32 = jnp.float32


def _mm(a, b):
    return jnp.dot(a.astype(BF16), b.astype(BF16), preferred_element_type=F32)


def _mm_nt(a, b):
    return lax.dot_general(a.astype(BF16), b.astype(BF16), (((1,), (1,)), ((), ())),
                           preferred_element_type=F32)


def _mm_tn(a, b):
    return lax.dot_general(a.astype(BF16), b.astype(BF16), (((0,), (0,)), ((), ())),
                           preferred_element_type=F32)


def _silu(x):
    return x * jax.nn.sigmoid(x)


def _log_sigmoid(x):
    return jnp.minimum(x, 0.0) - jnp.log1p(jnp.exp(-jnp.abs(x)))


def _rms(x, w):
    return x * lax.rsqrt(jnp.mean(x * x, axis=-1, keepdims=True) + EPS) * w


def _tril_ones(n):
    r = lax.broadcasted_iota(jnp.int32, (n, n), 0)
    c = lax.broadcasted_iota(jnp.int32, (n, n), 1)
    return r >= c


def _cumsum_rows(la, tril_bf16):
    hi = la.astype(BF16)
    lo = (la - hi.astype(F32)).astype(BF16)
    return (jnp.dot(tril_bf16, hi, preferred_element_type=F32)
            + jnp.dot(tril_bf16, lo, preferred_element_type=F32))


def _gla_decay_operands(q, k, b):
    c = q.shape[0]
    mid = b[c // 2:c // 2 + 1, :]
    last = b[c - 1:c, :]
    qd = q * (GLA_DK ** -0.5) * jnp.exp(b - mid)
    kd = k * jnp.exp(mid - b)
    qs = qd * jnp.exp(mid)
    kl = kd * jnp.exp(last - mid)
    return qd.astype(BF16), kd.astype(BF16), qs.astype(BF16), kl.astype(BF16), jnp.exp(last)


def _head(x, h, width):
    return x[:, h * width:(h + 1) * width]


def _gla_scores(qd, kd, tril_mask):
    return [jnp.where(tril_mask, _mm_nt(_head(qd, h, GLA_DK), _head(kd, h, GLA_DK)), 0.0)
            .astype(BF16) for h in range(GLA_HEADS)]


def _gla_state_step(kl, v, e_last):
    steps = []
    for h in range(GLA_HEADS):
        dec = jnp.broadcast_to(_head(e_last, h, GLA_DK), (GLA_DK, GLA_DK)).T
        dec = jnp.concatenate([dec] * (GLA_DV // GLA_DK), axis=1)
        steps.append((dec, _mm_tn(_head(kl, h, GLA_DK), _head(v, h, GLA_DV))))
    return steps


def _gla_outputs(qs, scores, v, states):
    outs = []
    for h in range(GLA_HEADS):
        lhs = jnp.concatenate([_head(qs, h, GLA_DK), scores[h]], axis=1)
        rhs = jnp.concatenate([states[h].astype(BF16), _head(v, h, GLA_DV)], axis=0)
        outs.append(jnp.dot(lhs, rhs, preferred_element_type=F32))
    return outs


def _gla_chunk(q, k, v, la, states, tril_mask, tril_bf16):
    v = v.astype(BF16)
    qd, kd, qs, kl, e_last = _gla_decay_operands(q, k, _cumsum_rows(la, tril_bf16))
    outs = _gla_outputs(qs, _gla_scores(qd, kd, tril_mask), v, states)
    steps = _gla_state_step(kl, v, e_last)
    return outs, [states[h] * steps[h][0] + steps[h][1] for h in range(GLA_HEADS)]


def _head_rms_gate(o_heads, r, gla_norm_w):
    cols = []
    for h in range(GLA_HEADS):
        vs = slice(h * GLA_DV, (h + 1) * GLA_DV)
        cols.append(_rms(o_heads[h], gla_norm_w) * _silu(r[:, vs]))
    return jnp.concatenate(cols, axis=1)


def _layernorm(x, w, b):
    mu = jnp.mean(x, axis=-1, keepdims=True)
    xc = x - mu
    var = jnp.mean(xc * xc, axis=-1, keepdims=True)
    return xc * lax.rsqrt(var + EPS) * w + b


def _spatial_gate(vn, ws_bf16, bs_cols):
    cols = []
    for g in range(GMLP_GROUPS):
        gs = slice(g * GMLP_DG, (g + 1) * GMLP_DG)
        cols.append(_mm(ws_bf16[g], vn[:, gs]) + bs_cols[:, g:g + 1])
    return jnp.concatenate(cols, axis=1)


def _masked_ws(ws, c):
    r = lax.broadcasted_iota(jnp.int32, (c, c), 0) // GLA_CHUNK
    s = lax.broadcasted_iota(jnp.int32, (c, c), 1) // GLA_CHUNK
    return jnp.where((r >= s)[None], ws, 0.0)


def _ada_kernel(c_ref, w_ref, b_ref, o_ref):
    o_ref[...] = _mm(_silu(c_ref[...]), w_ref[...]) + b_ref[...]


def _ada_call(c_all, w_ada, b_ada):
    rows = c_all.shape[0]
    return pl.pallas_call(
        _ada_kernel,
        grid=(3 * D_MODEL // ADA_COLS,),
        in_specs=[pl.BlockSpec((rows, D_MODEL), lambda j: (0, 0)),
                  pl.BlockSpec((D_MODEL, ADA_COLS), lambda j: (0, j)),
                  pl.BlockSpec((1, ADA_COLS), lambda j: (0, j))],
        out_specs=pl.BlockSpec((rows, ADA_COLS), lambda j: (0, j)),
        out_shape=jax.ShapeDtypeStruct((rows, 3 * D_MODEL), F32),
        name="ada_proj",
    )(c_all, w_ada, b_ada)


def _modulated_input(x, shift, scale, norm_w):
    return (_rms(x, norm_w) * (1.0 + scale) + shift).astype(BF16)


def _proj(h, w_ref, off, width):
    return jnp.dot(h, w_ref[:, off:off + width], preferred_element_type=F32)


def _log_decay(h, w_alr_ref, w_a2_ref, b_a):
    a_lr = jnp.dot(h, w_alr_ref[...], preferred_element_type=F32)
    return _log_sigmoid(_mm(a_lr, w_a2_ref[...]) + b_a) * (1.0 / GLA_TAU)


def _merge_out(x, gate, y_a, y_b, g_a, g_b, b_gate, w_out_ref, final_w):
    merged = (jax.nn.sigmoid(g_a + b_gate[0:1, :]) * y_a
              + jax.nn.sigmoid(g_b + b_gate[1:2, :]) * y_b)
    out = x + gate * _mm(merged, w_out_ref[...])
    return _rms(out, final_w)


def _prompt_kernel(x_ref, ada_ref, norm_w_ref, b_a_ref, gla_nw_ref, lnw_ref, lnb_ref, ws_ref,
                   bs_ref, bgate_ref, final_w_ref, w_main_ref, w_alr_ref, w_a2_ref, w_pa_ref,
                   w_pb_ref, w_out_ref, y_ref, state_ref, st_scr, ws_scr):
    i = pl.program_id(0)
    tile = x_ref.shape[0]

    @pl.when(i == 0)
    def _init():
        st_scr[...] = jnp.zeros_like(st_scr)
        ws_scr[...] = _masked_ws(ws_ref[...], GMLP_CHUNK).astype(BF16)

    x = x_ref[...]
    shift = ada_ref[:, 0:D_MODEL]
    scale = ada_ref[:, D_MODEL:2 * D_MODEL]
    gate = ada_ref[:, 2 * D_MODEL:3 * D_MODEL]
    h = _modulated_input(x, shift, scale, norm_w_ref[...])

    chunks = [slice(c * GLA_CHUNK, (c + 1) * GLA_CHUNK) for c in range(tile // GLA_CHUNK)]
    tril_mask = _tril_ones(GLA_CHUNK)
    tril_bf16 = tril_mask.astype(BF16)
    la = _log_decay(h, w_alr_ref, w_a2_ref, b_a_ref[...])
    q = _proj(h, w_main_ref, _OFF_Q, GLA_KEY)
    b = [_cumsum_rows(la[rows], tril_bf16) for rows in chunks]
    k = _proj(h, w_main_ref, _OFF_K, GLA_KEY)
    v = _proj(h, w_main_ref, _OFF_V, GLA_VAL).astype(BF16)
    ops = [_gla_decay_operands(q[rows], k[rows], b[c]) for c, rows in enumerate(chunks)]
    r = _proj(h, w_main_ref, _OFF_R, GLA_VAL)
    scores = [_gla_scores(ops[c][0], ops[c][1], tril_mask) for c in range(len(chunks))]
    steps = [_gla_state_step(ops[c][3], v[rows], ops[c][4]) for c, rows in enumerate(chunks)]
    gv = _proj(h, w_main_ref, _OFF_GV, GMLP_WIDTH)
    states = [[st_scr[hd] for hd in range(GLA_HEADS)]]
    for c in range(len(chunks)):
        states.append([states[c][hd] * steps[c][hd][0] + steps[c][hd][1]
                       for hd in range(GLA_HEADS)])
    for hd in range(GLA_HEADS):
        st_scr[hd] = states[-1][hd]
    o = [_gla_outputs(ops[c][2], scores[c], v[rows], states[c]) for c, rows in enumerate(chunks)]
    u = _proj(h, w_main_ref, _OFF_U, GMLP_WIDTH)

    vn = _layernorm(gv, lnw_ref[...], lnb_ref[...])
    ws = ws_scr[...]
    s = jnp.concatenate(
        [_spatial_gate(vn[j * GMLP_CHUNK:(j + 1) * GMLP_CHUNK], ws, bs_ref[...])
         for j in range(tile // GMLP_CHUNK)], axis=0)
    z = _proj(h, w_main_ref, _OFF_Z, GMLP_WIDTH)
    gated = jnp.concatenate(
        [_head_rms_gate(o[c], r[rows], gla_nw_ref[...]) for c, rows in enumerate(chunks)], axis=0)
    y_a = _mm(gated, w_pa_ref[...])
    g_a = _proj(h, w_main_ref, _OFF_GA, D_MODEL)
    y_b = _mm(u * s * _silu(z), w_pb_ref[...])
    g_b = _proj(h, w_main_ref, _OFF_GB, D_MODEL)

    y_ref[...] = _merge_out(x, gate, y_a, y_b, g_a, g_b, bgate_ref[...], w_out_ref,
                            final_w_ref[...])

    @pl.when(i == pl.num_programs(0) - 1)
    def _emit_state():
        for hd in range(GLA_HEADS):
            state_ref[hd] = states[-1][hd]


def _const_spec(shape):
    nd = len(shape)
    return pl.BlockSpec(shape, lambda i: (0,) * nd, pipeline_mode=pl.Buffered(1))


def _prompt_call(x, ada, small, weights):
    seq = x.shape[0]
    tile = PROMPT_TILE
    in_specs = ([pl.BlockSpec((tile, D_MODEL), lambda i: (i, 0)), _const_spec(ada.shape)]
                + [_const_spec(a.shape) for a in small]
                + [_const_spec(w.shape) for w in weights])
    return pl.pallas_call(
        _prompt_kernel,
        grid=(seq // tile,),
        in_specs=in_specs,
        out_specs=[pl.BlockSpec((tile, D_MODEL), lambda i: (i, 0)),
                   pl.BlockSpec((GLA_HEADS, GLA_DK, GLA_DV), lambda i: (0, 0, 0))],
        out_shape=[jax.ShapeDtypeStruct((seq, D_MODEL), F32),
                   jax.ShapeDtypeStruct((GLA_HEADS, GLA_DK, GLA_DV), F32)],
        scratch_shapes=[pltpu.VMEM((GLA_HEADS, GLA_DK, GLA_DV), F32),
                        pltpu.VMEM((GMLP_GROUPS, GMLP_CHUNK, GMLP_CHUNK), BF16)],
        compiler_params=pltpu.CompilerParams(dimension_semantics=("arbitrary",),
                                             vmem_limit_bytes=VMEM_LIMIT_BYTES),
        name="prompt_layer",
    )(x, ada, *small, *weights)


def _sample_kernel(x_ref, ada_ref, s0_ref, norm_w_ref, b_a_ref, gla_nw_ref, lnw_ref, lnb_ref,
                   ws_ref, bs_ref, bgate_ref, final_w_ref, w_main_ref, w_alr_ref, w_a2_ref,
                   w_pa_ref, w_pb_ref, w_out_ref, y_ref, state_ref, vn_ref,
                   q_scr, k_scr, v_scr, r_scr, la_scr, gated_scr, s_scr):
    nb, ns, _ = x_ref.shape
    rows = nb * ns
    x3 = x_ref[...]
    shift = ada_ref[:, :, 0:D_MODEL]
    scale = ada_ref[:, :, D_MODEL:2 * D_MODEL]
    gate = ada_ref[:, :, 2 * D_MODEL:3 * D_MODEL]
    h = _modulated_input(x3, shift, scale, norm_w_ref[...]).reshape(rows, D_MODEL)

    la_scr[...] = _log_decay(h, w_alr_ref, w_a2_ref, b_a_ref[...])
    q_scr[...] = _proj(h, w_main_ref, _OFF_Q, GLA_KEY)
    k_scr[...] = _proj(h, w_main_ref, _OFF_K, GLA_KEY)
    v_scr[...] = _proj(h, w_main_ref, _OFF_V, GLA_VAL)
    r_scr[...] = _proj(h, w_main_ref, _OFF_R, GLA_VAL)
    vn = _layernorm(_proj(h, w_main_ref, _OFF_GV, GMLP_WIDTH), lnw_ref[...], lnb_ref[...])
    vn_ref[...] = vn.reshape(nb, ns, GMLP_WIDTH)

    tril_mask = _tril_ones(ns)
    tril_bf16 = tril_mask.astype(BF16)
    ws = _masked_ws(ws_ref[...], ns).astype(BF16)
    bs_cols = bs_ref[...]
    gla_nw = gla_nw_ref[...]

    def stream(b, carry):
        rs = pl.ds(pl.multiple_of(b * ns, ns), ns)
        states = [s0_ref[b, hd] for hd in range(GLA_HEADS)]
        o_heads, states = _gla_chunk(q_scr[rs, :], k_scr[rs, :], v_scr[rs, :], la_scr[rs, :],
                                     states, tril_mask, tril_bf16)
        for hd in range(GLA_HEADS):
            state_ref[b, hd] = states[hd]
        gated_scr[rs, :] = _head_rms_gate(o_heads, r_scr[rs, :], gla_nw)
        s_scr[rs, :] = _spatial_gate(vn_ref[b], ws, bs_cols)
        return carry

    lax.fori_loop(0, nb, stream, 0)

    y_a = _mm(gated_scr[...], w_pa_ref[...])
    u = _proj(h, w_main_ref, _OFF_U, GMLP_WIDTH)
    z = _proj(h, w_main_ref, _OFF_Z, GMLP_WIDTH)
    y_b = _mm(u * s_scr[...] * _silu(z), w_pb_ref[...])
    g_a = _proj(h, w_main_ref, _OFF_GA, D_MODEL)
    g_b = _proj(h, w_main_ref, _OFF_GB, D_MODEL)
    merged = (jax.nn.sigmoid(g_a + bgate_ref[0:1, :]) * y_a
              + jax.nn.sigmoid(g_b + bgate_ref[1:2, :]) * y_b)
    upd = _mm(merged, w_out_ref[...]).reshape(nb, ns, D_MODEL)
    y_ref[...] = _rms(x3 + gate * upd, final_w_ref[...])


def _sample_call(x, ada, s0, small, weights):
    nb, ns, _ = x.shape
    rows = nb * ns
    args = (x, ada, s0, *small, *weights)
    full = lambda a: pl.BlockSpec(a.shape, lambda i, nd=a.ndim: (0,) * nd)
    return pl.pallas_call(
        _sample_kernel,
        grid=(1,),
        in_specs=[full(a) for a in args],
        out_specs=[pl.BlockSpec((nb, ns, D_MODEL), lambda i: (0, 0, 0)),
                   pl.BlockSpec(s0.shape, lambda i: (0, 0, 0, 0)),
                   pl.BlockSpec((nb, ns, GMLP_WIDTH), lambda i: (0, 0, 0))],
        out_shape=[jax.ShapeDtypeStruct((nb, ns, D_MODEL), F32),
                   jax.ShapeDtypeStruct(s0.shape, F32),
                   jax.ShapeDtypeStruct((nb, ns, GMLP_WIDTH), F32)],
        scratch_shapes=[pltpu.VMEM((rows, GLA_KEY), F32), pltpu.VMEM((rows, GLA_KEY), F32),
                        pltpu.VMEM((rows, GLA_VAL), F32), pltpu.VMEM((rows, GLA_VAL), F32),
                        pltpu.VMEM((rows, GLA_KEY), F32), pltpu.VMEM((rows, GLA_VAL), F32),
                        pltpu.VMEM((rows, GMLP_WIDTH), F32)],
        compiler_params=pltpu.CompilerParams(dimension_semantics=("arbitrary",),
                                             vmem_limit_bytes=VMEM_LIMIT_BYTES),
        name="sample_layer",
    )(*args)


def kernel(x_prompt, x_sample, state_gla, c_prompt, c_sample, norm_w, w_ada, b_ada, w_in, w_a2,
           b_a, gla_norm_w, ln_v_w, ln_v_b, w_s, b_s, b_gate, w_proj_a, w_proj_b, w_out,
           final_norm_w):
    depth = norm_w.shape[0]
    assert depth == 1 and x_prompt.shape[0] == 1
    nb, ns, _ = x_sample.shape
    lyr = 0

    c_all = jnp.concatenate([c_sample, c_prompt], axis=0)
    pad = (-c_all.shape[0]) % 8
    c_all = jnp.pad(c_all, ((0, pad), (0, 0)))
    ada = _ada_call(c_all, w_ada[lyr], b_ada[lyr][None, :])
    ada_sample = ada[:nb][:, None, :]
    ada_prompt = ada[nb:nb + 1]

    a0 = 2 * GLA_KEY + 2 * GLA_VAL
    w = w_in[lyr]
    w_main = jnp.concatenate([w[:, :a0], w[:, a0 + GLA_LOWRANK:]], axis=1).astype(BF16)
    w_alr = jnp.pad(w[:, a0:a0 + GLA_LOWRANK], ((0, 0), (0, LANES - GLA_LOWRANK))).astype(BF16)
    w_a2p = jnp.pad(w_a2[lyr], ((0, LANES - GLA_LOWRANK), (0, 0))).astype(BF16)
    weights = (w_main, w_alr, w_a2p, w_proj_a[lyr].astype(BF16), w_proj_b[lyr].astype(BF16),
               w_out[lyr].astype(BF16))

    def small(chunk):
        return (norm_w[lyr][None, :], b_a[lyr][None, :], gla_norm_w[lyr][None, :],
                ln_v_w[lyr][None, :], ln_v_b[lyr][None, :], w_s[lyr][:, :chunk, :chunk],
                b_s[lyr][:, :chunk].T, b_gate[lyr], final_norm_w[None, :])

    y_p, st_p = _prompt_call(x_prompt[0], ada_prompt, small(GMLP_CHUNK), weights)
    y_s, st_s, vn_s = _sample_call(x_sample, ada_sample, state_gla[lyr], small(min(GMLP_CHUNK, ns)),
                                   weights)
    return (y_p[None], y_s, st_p[None, None], st_s[None], vn_s[None])
```

```python
import functools

import jax
import jax.numpy as jnp
from jax import lax
from jax.experimental import pallas as pl
from jax.experimental.pallas import tpu as pltpu

D_MODEL = 1024
GLA_HEADS = 4
GLA_DK = 128
GLA_DV = 256
GLA_KEY = GLA_HEADS * GLA_DK
GLA_VAL = GLA_HEADS * GLA_DV
GLA_LOWRANK = 16
GLA_TAU = 16.0
GLA_CHUNK = 64
GMLP_CHUNK = 128
GMLP_GROUPS = 4
GMLP_WIDTH = 1024
GMLP_DG = GMLP_WIDTH // GMLP_GROUPS
EPS = 1e-6

LANES = 128
PROMPT_TILE = 256
ADA_COLS = 512
VMEM_LIMIT_BYTES = 56 * 1024 * 1024

_OFF_Q = 0
_OFF_K = _OFF_Q + GLA_KEY
_OFF_V = _OFF_K + GLA_KEY
_OFF_R = _OFF_V + GLA_VAL
_OFF_U = _OFF_R + GLA_VAL
_OFF_GV = _OFF_U + GMLP_WIDTH
_OFF_Z = _OFF_GV + GMLP_WIDTH
_OFF_GA = _OFF_Z + GMLP_WIDTH
_OFF_GB = _OFF_GA + D_MODEL
MAIN_COLS = _OFF_GB + D_MODEL

BF16 = jnp.bfloat16
F32 = jnp.float32


def _mm(a, b):
    return jnp.dot(a.astype(BF16), b.astype(BF16), preferred_element_type=F32)


def _mm_nt(a, b):
    return lax.dot_general(a.astype(BF16), b.astype(BF16), (((1,), (1,)), ((), ())),
                           preferred_element_type=F32)


def _mm_tn(a, b):
    return lax.dot_general(a.astype(BF16), b.astype(BF16), (((0,), (0,)), ((), ())),
                           preferred_element_type=F32)


def _silu(x):
    return x * jax.nn.sigmoid(x)


def _log_sigmoid(x):
    return jnp.minimum(x, 0.0) - jnp.log1p(jnp.exp(-jnp.abs(x)))


def _rms(x, w):
    return x * lax.rsqrt(jnp.mean(x * x, axis=-1, keepdims=True) + EPS) * w


def _tril_ones(n):
    r = lax.broadcasted_iota(jnp.int32, (n, n), 0)
    c = lax.broadcasted_iota(jnp.int32, (n, n), 1)
    return r >= c


def _cumsum_rows(la, tril_bf16):
    hi = la.astype(BF16)
    lo = (la - hi.astype(F32)).astype(BF16)
    return (jnp.dot(tril_bf16, hi, preferred_element_type=F32)
            + jnp.dot(tril_bf16, lo, preferred_element_type=F32))


def _gla_decay_operands(q, k, b):
    c = q.shape[0]
    mid = b[c // 2:c // 2 + 1, :]
    last = b[c - 1:c, :]
    qd = q * (GLA_DK ** -0.5) * jnp.exp(b - mid)
    kd = k * jnp.exp(mid - b)
    qs = qd * jnp.exp(mid)
    kl = kd * jnp.exp(last - mid)
    return qd.astype(BF16), kd.astype(BF16), qs.astype(BF16), kl.astype(BF16), jnp.exp(last)


def _head(x, h, width):
    return x[:, h * width:(h + 1) * width]


def _gla_scores(qd, kd, tril_mask):
    return [jnp.where(tril_mask, _mm_nt(_head(qd, h, GLA_DK), _head(kd, h, GLA_DK)), 0.0)
            .astype(BF16) for h in range(GLA_HEADS)]


def _gla_state_step(kl, v, e_last):
    steps = []
    for h in range(GLA_HEADS):
        dec = jnp.broadcast_to(_head(e_last, h, GLA_DK), (GLA_DK, GLA_DK)).T
        dec = jnp.concatenate([dec] * (GLA_DV // GLA_DK), axis=1)
        steps.append((dec, _mm_tn(_head(kl, h, GLA_DK), _head(v, h, GLA_DV))))
    return steps


def _gla_outputs(qs, scores, v, states):
    outs = []
    for h in range(GLA_HEADS):
        lhs = jnp.concatenate([_head(qs, h, GLA_DK), scores[h]], axis=1)
        rhs = jnp.concatenate([states[h].astype(BF16), _head(v, h, GLA_DV)], axis=0)
        outs.append(jnp.dot(lhs, rhs, preferred_element_type=F32))
    return outs


def _gla_chunk(q, k, v, la, states, tril_mask, tril_bf16):
    v = v.astype(BF16)
    qd, kd, qs, kl, e_last = _gla_decay_operands(q, k, _cumsum_rows(la, tril_bf16))
    outs = _gla_outputs(qs, _gla_scores(qd, kd, tril_mask), v, states)
    steps = _gla_state_step(kl, v, e_last)
    return outs, [states[h] * steps[h][0] + steps[h][1] for h in range(GLA_HEADS)]


def _head_rms_gate(o_heads, r, gla_norm_w):
    cols = []
    for h in range(GLA_HEADS):
        vs = slice(h * GLA_DV, (h + 1) * GLA_DV)
        cols.append(_rms(o_heads[h], gla_norm_w) * _silu(r[:, vs]))
    return jnp.concatenate(cols, axis=1)


def _layernorm(x, w, b):
    mu = jnp.mean(x, axis=-1, keepdims=True)
    xc = x - mu
    var = jnp.mean(xc * xc, axis=-1, keepdims=True)
    return xc * lax.rsqrt(var + EPS) * w + b


def _spatial_gate(vn, ws_bf16, bs_cols):
    cols = []
    for g in range(GMLP_GROUPS):
        gs = slice(g * GMLP_DG, (g + 1) * GMLP_DG)
        cols.append(_mm(ws_bf16[g], vn[:, gs]) + bs_cols[:, g:g + 1])
    return jnp.concatenate(cols, axis=1)


def _masked_ws(ws, c):
    r = lax.broadcasted_iota(jnp.int32, (c, c), 0) // GLA_CHUNK
    s = lax.broadcasted_iota(jnp.int32, (c, c), 1) // GLA_CHUNK
    return jnp.where((r >= s)[None], ws, 0.0)


def _ada_kernel(c_ref, w_ref, b_ref, o_ref):
    o_ref[...] = _mm(_silu(c_ref[...]), w_ref[...]) + b_ref[...]


def _ada_call(c_all, w_ada, b_ada):
    rows = c_all.shape[0]
    return pl.pallas_call(
        _ada_kernel,
        grid=(3 * D_MODEL // ADA_COLS,),
        in_specs=[pl.BlockSpec((rows, D_MODEL), lambda j: (0, 0)),
                  pl.BlockSpec((D_MODEL, ADA_COLS), lambda j: (0, j)),
                  pl.BlockSpec((1, ADA_COLS), lambda j: (0, j))],
        out_specs=pl.BlockSpec((rows, ADA_COLS), lambda j: (0, j)),
        out_shape=jax.ShapeDtypeStruct((rows, 3 * D_MODEL), F32),
        name="ada_proj",
    )(c_all, w_ada, b_ada)


def _modulated_input(x, shift, scale, norm_w):
    return (_rms(x, norm_w) * (1.0 + scale) + shift).astype(BF16)


def _proj(h, w_ref, off, width):
    return jnp.dot(h, w_ref[:, off:off + width], preferred_element_type=F32)


def _log_decay(h, w_alr_ref, w_a2_ref, b_a):
    a_lr = jnp.dot(h, w_alr_ref[...], preferred_element_type=F32)
    return _log_sigmoid(_mm(a_lr, w_a2_ref[...]) + b_a) * (1.0 / GLA_TAU)


def _prompt_kernel(x_ref, xn_ref, ada_ref, norm_w_ref, b_a_ref, gla_nw_ref, lnw_ref, lnb_ref,
                   ws_ref, bs_ref, bgate_ref, final_w_ref, w_main_ref, w_alr_ref, w_a2_ref,
                   w_pa_ref, w_pb_ref, w_out_ref, y_ref, state_ref, st_scr, ws_scr, h_scr,
                   out_scr):
    i = pl.program_id(0)
    n_tiles = pl.num_programs(0) - 1
    slot = lax.rem(i, 2)

    def next_input(src_ref, dst_slot):
        h_scr[dst_slot] = _modulated_input(src_ref[...], ada_ref[:, 0:D_MODEL],
                                           ada_ref[:, D_MODEL:2 * D_MODEL], norm_w_ref[...])

    def previous_output():
        y_ref[...] = _rms(out_scr[...], final_w_ref[...])

    @pl.when(i == 0)
    def _fill():
        st_scr[...] = jnp.zeros_like(st_scr)
        out_scr[...] = jnp.zeros_like(out_scr)
        ws_scr[...] = _masked_ws(ws_ref[...], GMLP_CHUNK).astype(BF16)
        next_input(x_ref, 0)

    @pl.when(i < n_tiles)
    def _steady():
        _prompt_tile_body(x_ref, ada_ref, b_a_ref, gla_nw_ref, lnw_ref, lnb_ref, bs_ref,
                          bgate_ref, w_main_ref, w_alr_ref, w_a2_ref, w_pa_ref, w_pb_ref,
                          w_out_ref, st_scr, ws_scr, h_scr.at[slot], out_scr,
                          previous_output, functools.partial(next_input, xn_ref, 1 - slot))

    @pl.when(i == n_tiles)
    def _drain():
        previous_output()
        state_ref[...] = st_scr[...]


def _prompt_tile_body(x_ref, ada_ref, b_a_ref, gla_nw_ref, lnw_ref, lnb_ref, bs_ref, bgate_ref,
                      w_main_ref, w_alr_ref, w_a2_ref, w_pa_ref, w_pb_ref, w_out_ref, st_scr,
                      ws_scr, h_ref, out_scr, previous_output, next_input):
    tile = x_ref.shape[0]
    gate = ada_ref[:, 2 * D_MODEL:3 * D_MODEL]
    h = h_ref[...]

    chunks = [slice(c * GLA_CHUNK, (c + 1) * GLA_CHUNK) for c in range(tile // GLA_CHUNK)]
    tril_mask = _tril_ones(GLA_CHUNK)
    tril_bf16 = tril_mask.astype(BF16)
    la = _log_decay(h, w_alr_ref, w_a2_ref, b_a_ref[...])
    q = _proj(h, w_main_ref, _OFF_Q, GLA_KEY)
    b = [_cumsum_rows(la[rows], tril_bf16) for rows in chunks]
    k = _proj(h, w_main_ref, _OFF_K, GLA_KEY)
    previous_output()
    v = _proj(h, w_main_ref, _OFF_V, GLA_VAL).astype(BF16)
    ops = [_gla_decay_operands(q[rows], k[rows], b[c]) for c, rows in enumerate(chunks)]
    r = _proj(h, w_main_ref, _OFF_R, GLA_VAL)
    scores = [_gla_scores(ops[c][0], ops[c][1], tril_mask) for c in range(len(chunks))]
    steps = [_gla_state_step(ops[c][3], v[rows], ops[c][4]) for c, rows in enumerate(chunks)]
    gv = _proj(h, w_main_ref, _OFF_GV, GMLP_WIDTH)
    states = [[st_scr[hd] for hd in range(GLA_HEADS)]]
    for c in range(len(chunks)):
        states.append([states[c][hd] * steps[c][hd][0] + steps[c][hd][1]
                       for hd in range(GLA_HEADS)])
    for hd in range(GLA_HEADS):
        st_scr[hd] = states[-1][hd]
    o = [_gla_outputs(ops[c][2], scores[c], v[rows], states[c]) for c, rows in enumerate(chunks)]
    u = _proj(h, w_main_ref, _OFF_U, GMLP_WIDTH)

    vn = _layernorm(gv, lnw_ref[...], lnb_ref[...])
    ws = ws_scr[...]
    s = jnp.concatenate(
        [_spatial_gate(vn[j * GMLP_CHUNK:(j + 1) * GMLP_CHUNK], ws, bs_ref[...])
         for j in range(tile // GMLP_CHUNK)], axis=0)
    z = _proj(h, w_main_ref, _OFF_Z, GMLP_WIDTH)
    gated = jnp.concatenate(
        [_head_rms_gate(o[c], r[rows], gla_nw_ref[...]) for c, rows in enumerate(chunks)], axis=0)
    g_a = _proj(h, w_main_ref, _OFF_GA, D_MODEL)
    g_b = _proj(h, w_main_ref, _OFF_GB, D_MODEL)
    y_a = _mm(gated, w_pa_ref[...])
    next_input()
    y_b = _mm(u * s * _silu(z), w_pb_ref[...])

    merged = (jax.nn.sigmoid(g_a + bgate_ref[0:1, :]) * y_a
              + jax.nn.sigmoid(g_b + bgate_ref[1:2, :]) * y_b)
    out_scr[...] = x_ref[...] + gate * _mm(merged, w_out_ref[...])


def _const_spec(shape):
    nd = len(shape)
    return pl.BlockSpec(shape, lambda i: (0,) * nd, pipeline_mode=pl.Buffered(1))


def _prompt_call(x, ada, small, weights):
    seq = x.shape[0]
    tile = PROMPT_TILE
    n_tiles = seq // tile
    last = n_tiles - 1
    in_specs = ([pl.BlockSpec((tile, D_MODEL), lambda i: (jnp.minimum(i, last), 0)),
                 pl.BlockSpec((tile, D_MODEL), lambda i: (jnp.minimum(i + 1, last), 0)),
                 _const_spec(ada.shape)]
                + [_const_spec(a.shape) for a in small]
                + [_const_spec(w.shape) for w in weights])
    return pl.pallas_call(
        _prompt_kernel,
        grid=(n_tiles + 1,),
        in_specs=in_specs,
        out_specs=[pl.BlockSpec((tile, D_MODEL), lambda i: (jnp.maximum(i - 1, 0), 0)),
                   pl.BlockSpec((GLA_HEADS, GLA_DK, GLA_DV), lambda i: (0, 0, 0))],
        out_shape=[jax.ShapeDtypeStruct((seq, D_MODEL), F32),
                   jax.ShapeDtypeStruct((GLA_HEADS, GLA_DK, GLA_DV), F32)],
        scratch_shapes=[pltpu.VMEM((GLA_HEADS, GLA_DK, GLA_DV), F32),
                        pltpu.VMEM((GMLP_GROUPS, GMLP_CHUNK, GMLP_CHUNK), BF16),
                        pltpu.VMEM((2, tile, D_MODEL), BF16),
                        pltpu.VMEM((tile, D_MODEL), F32)],
        compiler_params=pltpu.CompilerParams(dimension_semantics=("arbitrary",),
                                             vmem_limit_bytes=VMEM_LIMIT_BYTES),
        name="prompt_layer",
    )(x, x, ada, *small, *weights)


def _sample_kernel(x_ref, ada_ref, s0_ref, norm_w_ref, b_a_ref, gla_nw_ref, lnw_ref, lnb_ref,
                   ws_ref, bs_ref, bgate_ref, final_w_ref, w_main_ref, w_alr_ref, w_a2_ref,
                   w_pa_ref, w_pb_ref, w_out_ref, y_ref, state_ref, vn_ref,
                   q_scr, k_scr, v_scr, r_scr, la_scr, gated_scr, s_scr):
    nb, ns, _ = x_ref.shape
    rows = nb * ns
    x3 = x_ref[...]
    shift = ada_ref[:, :, 0:D_MODEL]
    scale = ada_ref[:, :, D_MODEL:2 * D_MODEL]
    gate = ada_ref[:, :, 2 * D_MODEL:3 * D_MODEL]
    h = _modulated_input(x3, shift, scale, norm_w_ref[...]).reshape(rows, D_MODEL)

    la_scr[...] = _log_decay(h, w_alr_ref, w_a2_ref, b_a_ref[...])
    q_scr[...] = _proj(h, w_main_ref, _OFF_Q, GLA_KEY)
    k_scr[...] = _proj(h, w_main_ref, _OFF_K, GLA_KEY)
    v_scr[...] = _proj(h, w_main_ref, _OFF_V, GLA_VAL)
    r_scr[...] = _proj(h, w_main_ref, _OFF_R, GLA_VAL)
    vn = _layernorm(_proj(h, w_main_ref, _OFF_GV, GMLP_WIDTH), lnw_ref[...], lnb_ref[...])
    vn_ref[...] = vn.reshape(nb, ns, GMLP_WIDTH)

    tril_mask = _tril_ones(ns)
    tril_bf16 = tril_mask.astype(BF16)
    ws = _masked_ws(ws_ref[...], ns).astype(BF16)
    bs_cols = bs_ref[...]
    gla_nw = gla_nw_ref[...]

    def stream(b, carry):
        rs = pl.ds(pl.multiple_of(b * ns, ns), ns)
        states = [s0_ref[b, hd] for hd in range(GLA_HEADS)]
        o_heads, states = _gla_chunk(q_scr[rs, :], k_scr[rs, :], v_scr[rs, :], la_scr[rs, :],
                                     states, tril_mask, tril_bf16)
        for hd in range(GLA_HEADS):
            state_ref[b, hd] = states[hd]
        gated_scr[rs, :] = _head_rms_gate(o_heads, r_scr[rs, :], gla_nw)
        s_scr[rs, :] = _spatial_gate(vn_ref[b], ws, bs_cols)
        return carry

    lax.fori_loop(0, nb, stream, 0)

    y_a = _mm(gated_scr[...], w_pa_ref[...])
    u = _proj(h, w_main_ref, _OFF_U, GMLP_WIDTH)
    z = _proj(h, w_main_ref, _OFF_Z, GMLP_WIDTH)
    y_b = _mm(u * s_scr[...] * _silu(z), w_pb_ref[...])
    g_a = _proj(h, w_main_ref, _OFF_GA, D_MODEL)
    g_b = _proj(h, w_main_ref, _OFF_GB, D_MODEL)
    merged = (jax.nn.sigmoid(g_a + bgate_ref[0:1, :]) * y_a
              + jax.nn.sigmoid(g_b + bgate_ref[1:2, :]) * y_b)
    upd = _mm(merged, w_out_ref[...]).reshape(nb, ns, D_MODEL)
    y_ref[...] = _rms(x3 + gate * upd, final_w_ref[...])


def _sample_call(x, ada, s0, small, weights):
    nb, ns, _ = x.shape
    rows = nb * ns
    args = (x, ada, s0, *small, *weights)
    full = lambda a: pl.BlockSpec(a.shape, lambda i, nd=a.ndim: (0,) * nd)
    return pl.pallas_call(
        _sample_kernel,
        grid=(1,),
        in_specs=[full(a) for a in args],
        out_specs=[pl.BlockSpec((nb, ns, D_MODEL), lambda i: (0, 0, 0)),
                   pl.BlockSpec(s0.shape, lambda i: (0, 0, 0, 0)),
                   pl.BlockSpec((nb, ns, GMLP_WIDTH), lambda i: (0, 0, 0))],
        out_shape=[jax.ShapeDtypeStruct((nb, ns, D_MODEL), F32),
                   jax.ShapeDtypeStruct(s0.shape, F32),
                   jax.ShapeDtypeStruct((nb, ns, GMLP_WIDTH), F32)],
        scratch_shapes=[pltpu.VMEM((rows, GLA_KEY), F32), pltpu.VMEM((rows, GLA_KEY), F32),
                        pltpu.VMEM((rows, GLA_VAL), F32), pltpu.VMEM((rows, GLA_VAL), F32),
                        pltpu.VMEM((rows, GLA_KEY), F32), pltpu.VMEM((rows, GLA_VAL), F32),
                        pltpu.VMEM((rows, GMLP_WIDTH), F32)],
        compiler_params=pltpu.CompilerParams(dimension_semantics=("arbitrary",),
                                             vmem_limit_bytes=VMEM_LIMIT_BYTES),
        name="sample_layer",
    )(*args)


def kernel(x_prompt, x_sample, state_gla, c_prompt, c_sample, norm_w, w_ada, b_ada, w_in, w_a2,
           b_a, gla_norm_w, ln_v_w, ln_v_b, w_s, b_s, b_gate, w_proj_a, w_proj_b, w_out,
           final_norm_w):
    depth = norm_w.shape[0]
    assert depth == 1 and x_prompt.shape[0] == 1
    nb, ns, _ = x_sample.shape
    lyr = 0

    c_all = jnp.concatenate([c_sample, c_prompt], axis=0)
    pad = (-c_all.shape[0]) % 8
    c_all = jnp.pad(c_all, ((0, pad), (0, 0)))
    ada = _ada_call(c_all, w_ada[lyr], b_ada[lyr][None, :])
    ada_sample = ada[:nb][:, None, :]
    ada_prompt = ada[nb:nb + 1]

    a0 = 2 * GLA_KEY + 2 * GLA_VAL
    w = w_in[lyr]
    w_main = jnp.concatenate([w[:, :a0], w[:, a0 + GLA_LOWRANK:]], axis=1).astype(BF16)
    w_alr = jnp.pad(w[:, a0:a0 + GLA_LOWRANK], ((0, 0), (0, LANES - GLA_LOWRANK))).astype(BF16)
    w_a2p = jnp.pad(w_a2[lyr], ((0, LANES - GLA_LOWRANK), (0, 0))).astype(BF16)
    weights = (w_main, w_alr, w_a2p, w_proj_a[lyr].astype(BF16), w_proj_b[lyr].astype(BF16),
               w_out[lyr].astype(BF16))

    def small(chunk):
        return (norm_w[lyr][None, :], b_a[lyr][None, :], gla_norm_w[lyr][None, :],
                ln_v_w[lyr][None, :], ln_v_b[lyr][None, :], w_s[lyr][:, :chunk, :chunk],
                b_s[lyr][:, :chunk].T, b_gate[lyr], final_norm_w[None, :])

    y_p, st_p = _prompt_call(x_prompt[0], ada_prompt, small(GMLP_CHUNK), weights)
    y_s, st_s, vn_s = _sample_call(x_sample, ada_sample, state_gla[lyr], small(min(GMLP_CHUNK, ns)),
                                   weights)
    return (y_p[None], y_s, st_p[None, None], st_s[None], vn_s[None])
```

```python
import functools

import jax
import jax.numpy as jnp
from jax import lax
from jax.experimental import pallas as pl
from jax.experimental.pallas import tpu as pltpu

D_MODEL = 1024
GLA_HEADS = 4
GLA_DK = 128
GLA_DV = 256
GLA_KEY = GLA_HEADS * GLA_DK
GLA_VAL = GLA_HEADS * GLA_DV
GLA_LOWRANK = 16
GLA_TAU = 16.0
GLA_CHUNK = 64
GMLP_CHUNK = 128
GMLP_GROUPS = 4
GMLP_WIDTH = 1024
GMLP_DG = GMLP_WIDTH // GMLP_GROUPS
EPS = 1e-6

LANES = 128
PROMPT_TILE = 256
ADA_COLS = 512
REPACK_ROWS = 512
VMEM_LIMIT_BYTES = 56 * 1024 * 1024

_OFF_Q = 0
_OFF_K = _OFF_Q + GLA_KEY
_OFF_V = _OFF_K + GLA_KEY
_OFF_R = _OFF_V + GLA_VAL
_OFF_U = _OFF_R + GLA_VAL
_OFF_GV = _OFF_U + GMLP_WIDTH
_OFF_Z = _OFF_GV + GMLP_WIDTH
_OFF_GA = _OFF_Z + GMLP_WIDTH
_OFF_GB = _OFF_GA + D_MODEL
MAIN_COLS = _OFF_GB + D_MODEL

BF16 = jnp.bfloat16
F32 = jnp.float32


def _mm(a, b):
    return jnp.dot(a.astype(BF16), b.astype(BF16), preferred_element_type=F32)


def _mm_nt(a, b):
    return lax.dot_general(a.astype(BF16), b.astype(BF16), (((1,), (1,)), ((), ())),
                           preferred_element_type=F32)


def _mm_tn(a, b):
    return lax.dot_general(a.astype(BF16), b.astype(BF16), (((0,), (0,)), ((), ())),
                           preferred_element_type=F32)


def _silu(x):
    return x * jax.nn.sigmoid(x)


def _log_sigmoid(x):
    return jnp.minimum(x, 0.0) - jnp.log1p(jnp.exp(-jnp.abs(x)))


def _rms(x, w):
    return x * lax.rsqrt(jnp.mean(x * x, axis=-1, keepdims=True) + EPS) * w


def _tril_ones(n):
    r = lax.broadcasted_iota(jnp.int32, (n, n), 0)
    c = lax.broadcasted_iota(jnp.int32, (n, n), 1)
    return r >= c


def _cumsum_rows(la, tril_bf16):
    hi = la.astype(BF16)
    lo = (la - hi.astype(F32)).astype(BF16)
    return (jnp.dot(tril_bf16, hi, preferred_element_type=F32)
            + jnp.dot(tril_bf16, lo, preferred_element_type=F32))


def _gla_decay_operands(q, k, b):
    c = q.shape[0]
    mid = b[c // 2:c // 2 + 1, :]
    last = b[c - 1:c, :]
    qd = q * (GLA_DK ** -0.5) * jnp.exp(b - mid)
    kd = k * jnp.exp(mid - b)
    qs = qd * jnp.exp(mid)
    kl = kd * jnp.exp(last - mid)
    return qd.astype(BF16), kd.astype(BF16), qs.astype(BF16), kl.astype(BF16), jnp.exp(last)


def _head(x, h, width):
    return x[:, h * width:(h + 1) * width]


def _gla_scores(qd, kd, tril_mask):
    return [jnp.where(tril_mask, _mm_nt(_head(qd, h, GLA_DK), _head(kd, h, GLA_DK)), 0.0)
            .astype(BF16) for h in range(GLA_HEADS)]


def _gla_state_step(kl, v, e_last):
    steps = []
    for h in range(GLA_HEADS):
        dec = jnp.broadcast_to(_head(e_last, h, GLA_DK), (GLA_DK, GLA_DK)).T
        dec = jnp.concatenate([dec] * (GLA_DV // GLA_DK), axis=1)
        steps.append((dec, _mm_tn(_head(kl, h, GLA_DK), _head(v, h, GLA_DV))))
    return steps


def _gla_outputs(qs, scores, v, states):
    outs = []
    for h in range(GLA_HEADS):
        lhs = jnp.concatenate([_head(qs, h, GLA_DK), scores[h]], axis=1)
        rhs = jnp.concatenate([states[h].astype(BF16), _head(v, h, GLA_DV)], axis=0)
        outs.append(jnp.dot(lhs, rhs, preferred_element_type=F32))
    return outs


def _gla_chunk(q, k, v, la, states, tril_mask, tril_bf16):
    v = v.astype(BF16)
    qd, kd, qs, kl, e_last = _gla_decay_operands(q, k, _cumsum_rows(la, tril_bf16))
    outs = _gla_outputs(qs, _gla_scores(qd, kd, tril_mask), v, states)
    steps = _gla_state_step(kl, v, e_last)
    return outs, [states[h] * steps[h][0] + steps[h][1] for h in range(GLA_HEADS)]


def _head_rms_gate(o_heads, r, gla_norm_w):
    cols = []
    for h in range(GLA_HEADS):
        vs = slice(h * GLA_DV, (h + 1) * GLA_DV)
        cols.append(_rms(o_heads[h], gla_norm_w) * _silu(r[:, vs]))
    return jnp.concatenate(cols, axis=1)


def _layernorm(x, w, b):
    mu = jnp.mean(x, axis=-1, keepdims=True)
    xc = x - mu
    var = jnp.mean(xc * xc, axis=-1, keepdims=True)
    return xc * lax.rsqrt(var + EPS) * w + b


def _spatial_gate(vn, ws_bf16, bs_cols):
    cols = []
    for g in range(GMLP_GROUPS):
        gs = slice(g * GMLP_DG, (g + 1) * GMLP_DG)
        cols.append(_mm(ws_bf16[g], vn[:, gs]) + bs_cols[:, g:g + 1])
    return jnp.concatenate(cols, axis=1)


def _masked_ws(ws, c):
    r = lax.broadcasted_iota(jnp.int32, (c, c), 0) // GLA_CHUNK
    s = lax.broadcasted_iota(jnp.int32, (c, c), 1) // GLA_CHUNK
    return jnp.where((r >= s)[None], ws, 0.0)


def _ada_kernel(c_ref, w_ref, b_ref, o_ref):
    o_ref[...] = _mm(_silu(c_ref[...]), w_ref[...]) + b_ref[...]


def _ada_call(c_all, w_ada, b_ada):
    rows = c_all.shape[0]
    return pl.pallas_call(
        _ada_kernel,
        grid=(3 * D_MODEL // ADA_COLS,),
        in_specs=[pl.BlockSpec((rows, D_MODEL), lambda j: (0, 0)),
                  pl.BlockSpec((D_MODEL, ADA_COLS), lambda j: (0, j)),
                  pl.BlockSpec((1, ADA_COLS), lambda j: (0, j))],
        out_specs=pl.BlockSpec((rows, ADA_COLS), lambda j: (0, j)),
        out_shape=jax.ShapeDtypeStruct((rows, 3 * D_MODEL), F32),
        name="ada_proj",
    )(c_all, w_ada, b_ada)


def _repack_kernel(wt_ref, o_ref):
    o_ref[...] = wt_ref[...].T.astype(BF16)


def _repack_call(w_t, gap_start, gap):
    rows = REPACK_ROWS
    assert gap_start % rows == 0 and MAIN_COLS % rows == 0
    first_after_gap = gap_start // rows

    def src_row(j):
        return (j * (rows // gap) + jnp.where(j < first_after_gap, 0, 1)) * gap

    return pl.pallas_call(
        _repack_kernel,
        grid=(MAIN_COLS // rows,),
        in_specs=[pl.BlockSpec((pl.Element(rows), pl.Element(D_MODEL)),
                               lambda j: (src_row(j), 0))],
        out_specs=pl.BlockSpec((D_MODEL, rows), lambda j: (0, j)),
        out_shape=jax.ShapeDtypeStruct((D_MODEL, MAIN_COLS), BF16),
        name="repack_w_in",
    )(w_t)


def _modulated_input(x, shift, scale, norm_w):
    return (_rms(x, norm_w) * (1.0 + scale) + shift).astype(BF16)


def _proj(h, w_ref, off, width):
    return jnp.dot(h, w_ref[:, off:off + width], preferred_element_type=F32)


def _log_decay(h, w_alr_ref, w_a2_ref, b_a):
    a_lr = jnp.dot(h, w_alr_ref[...], preferred_element_type=F32)
    return _log_sigmoid(_mm(a_lr, w_a2_ref[...]) + b_a) * (1.0 / GLA_TAU)


def _prompt_kernel(x_ref, xn_ref, ada_ref, norm_w_ref, b_a_ref, gla_nw_ref, lnw_ref, lnb_ref,
                   ws_ref, bs_ref, bgate_ref, final_w_ref, w_main_ref, w_alr_ref, w_a2_ref,
                   w_pa_ref, w_pb_ref, w_out_ref, y_ref, state_ref, st_scr, ws_scr, h_scr,
                   out_scr):
    i = pl.program_id(0)
    n_tiles = pl.num_programs(0) - 1
    slot = lax.rem(i, 2)

    def next_input(src_ref, dst_slot):
        h_scr[dst_slot] = _modulated_input(src_ref[...], ada_ref[:, 0:D_MODEL],
                                           ada_ref[:, D_MODEL:2 * D_MODEL], norm_w_ref[...])

    def previous_output():
        y_ref[...] = _rms(out_scr[...], final_w_ref[...])

    @pl.when(i == 0)
    def _fill():
        st_scr[...] = jnp.zeros_like(st_scr)
        out_scr[...] = jnp.zeros_like(out_scr)
        ws_scr[...] = _masked_ws(ws_ref[...], GMLP_CHUNK).astype(BF16)
        next_input(x_ref, 0)

    @pl.when(i < n_tiles)
    def _steady():
        _prompt_tile_body(x_ref, ada_ref, b_a_ref, gla_nw_ref, lnw_ref, lnb_ref, bs_ref,
                          bgate_ref, w_main_ref, w_alr_ref, w_a2_ref, w_pa_ref, w_pb_ref,
                          w_out_ref, st_scr, ws_scr, h_scr.at[slot], out_scr,
                          previous_output, functools.partial(next_input, xn_ref, 1 - slot))

    @pl.when(i == n_tiles)
    def _drain():
        previous_output()
        state_ref[...] = st_scr[...]


def _prompt_tile_body(x_ref, ada_ref, b_a_ref, gla_nw_ref, lnw_ref, lnb_ref, bs_ref, bgate_ref,
                      w_main_ref, w_alr_ref, w_a2_ref, w_pa_ref, w_pb_ref, w_out_ref, st_scr,
                      ws_scr, h_ref, out_scr, previous_output, next_input):
    tile = x_ref.shape[0]
    gate = ada_ref[:, 2 * D_MODEL:3 * D_MODEL]
    h = h_ref[...]

    chunks = [slice(c * GLA_CHUNK, (c + 1) * GLA_CHUNK) for c in range(tile // GLA_CHUNK)]
    tril_mask = _tril_ones(GLA_CHUNK)
    tril_bf16 = tril_mask.astype(BF16)
    la = _log_decay(h, w_alr_ref, w_a2_ref, b_a_ref[...])
    q = _proj(h, w_main_ref, _OFF_Q, GLA_KEY)
    b = [_cumsum_rows(la[rows], tril_bf16) for rows in chunks]
    k = _proj(h, w_main_ref, _OFF_K, GLA_KEY)
    previous_output()
    v = _proj(h, w_main_ref, _OFF_V, GLA_VAL).astype(BF16)
    ops = [_gla_decay_operands(q[rows], k[rows], b[c]) for c, rows in enumerate(chunks)]
    r = _proj(h, w_main_ref, _OFF_R, GLA_VAL)
    scores = [_gla_scores(ops[c][0], ops[c][1], tril_mask) for c in range(len(chunks))]
    steps = [_gla_state_step(ops[c][3], v[rows], ops[c][4]) for c, rows in enumerate(chunks)]
    gv = _proj(h, w_main_ref, _OFF_GV, GMLP_WIDTH)
    states = [[st_scr[hd] for hd in range(GLA_HEADS)]]
    for c in range(len(chunks)):
        states.append([states[c][hd] * steps[c][hd][0] + steps[c][hd][1]
                       for hd in range(GLA_HEADS)])
    for hd in range(GLA_HEADS):
        st_scr[hd] = states[-1][hd]
    o = [_gla_outputs(ops[c][2], scores[c], v[rows], states[c]) for c, rows in enumerate(chunks)]
    u = _proj(h, w_main_ref, _OFF_U, GMLP_WIDTH)

    vn = _layernorm(gv, lnw_ref[...], lnb_ref[...])
    ws = ws_scr[...]
    s = jnp.concatenate(
        [_spatial_gate(vn[j * GMLP_CHUNK:(j + 1) * GMLP_CHUNK], ws, bs_ref[...])
         for j in range(tile // GMLP_CHUNK)], axis=0)
    z = _proj(h, w_main_ref, _OFF_Z, GMLP_WIDTH)
    gated = jnp.concatenate(
        [_head_rms_gate(o[c], r[rows], gla_nw_ref[...]) for c, rows in enumerate(chunks)], axis=0)
    g_a = _proj(h, w_main_ref, _OFF_GA, D_MODEL)
    g_b = _proj(h, w_main_ref, _OFF_GB, D_MODEL)
    y_a = _mm(gated, w_pa_ref[...])
    next_input()
    y_b = _mm(u * s * _silu(z), w_pb_ref[...])

    merged = (jax.nn.sigmoid(g_a + bgate_ref[0:1, :]) * y_a
              + jax.nn.sigmoid(g_b + bgate_ref[1:2, :]) * y_b)
    out_scr[...] = x_ref[...] + gate * _mm(merged, w_out_ref[...])


def _const_spec(shape):
    nd = len(shape)
    return pl.BlockSpec(shape, lambda i: (0,) * nd, pipeline_mode=pl.Buffered(1))


def _prompt_call(x, ada, small, weights):
    seq = x.shape[0]
    tile = PROMPT_TILE
    n_tiles = seq // tile
    last = n_tiles - 1
    in_specs = ([pl.BlockSpec((tile, D_MODEL), lambda i: (jnp.minimum(i, last), 0)),
                 pl.BlockSpec((tile, D_MODEL), lambda i: (jnp.minimum(i + 1, last), 0)),
                 _const_spec(ada.shape)]
                + [_const_spec(a.shape) for a in small]
                + [_const_spec(w.shape) for w in weights])
    return pl.pallas_call(
        _prompt_kernel,
        grid=(n_tiles + 1,),
        in_specs=in_specs,
        out_specs=[pl.BlockSpec((tile, D_MODEL), lambda i: (jnp.maximum(i - 1, 0), 0)),
                   pl.BlockSpec((GLA_HEADS, GLA_DK, GLA_DV), lambda i: (0, 0, 0))],
        out_shape=[jax.ShapeDtypeStruct((seq, D_MODEL), F32),
                   jax.ShapeDtypeStruct((GLA_HEADS, GLA_DK, GLA_DV), F32)],
        scratch_shapes=[pltpu.VMEM((GLA_HEADS, GLA_DK, GLA_DV), F32),
                        pltpu.VMEM((GMLP_GROUPS, GMLP_CHUNK, GMLP_CHUNK), BF16),
                        pltpu.VMEM((2, tile, D_MODEL), BF16),
                        pltpu.VMEM((tile, D_MODEL), F32)],
        compiler_params=pltpu.CompilerParams(dimension_semantics=("arbitrary",),
                                             vmem_limit_bytes=VMEM_LIMIT_BYTES),
        name="prompt_layer",
    )(x, x, ada, *small, *weights)


def _sample_kernel(x_ref, ada_ref, s0_ref, norm_w_ref, b_a_ref, gla_nw_ref, lnw_ref, lnb_ref,
                   ws_ref, bs_ref, bgate_ref, final_w_ref, w_main_ref, w_alr_ref, w_a2_ref,
                   w_pa_ref, w_pb_ref, w_out_ref, y_ref, state_ref, vn_ref,
                   q_scr, k_scr, v_scr, r_scr, la_scr, gated_scr, s_scr):
    nb, ns, _ = x_ref.shape
    rows = nb * ns
    x3 = x_ref[...]
    shift = ada_ref[:, :, 0:D_MODEL]
    scale = ada_ref[:, :, D_MODEL:2 * D_MODEL]
    gate = ada_ref[:, :, 2 * D_MODEL:3 * D_MODEL]
    h = _modulated_input(x3, shift, scale, norm_w_ref[...]).reshape(rows, D_MODEL)

    la_scr[...] = _log_decay(h, w_alr_ref, w_a2_ref, b_a_ref[...])
    q_scr[...] = _proj(h, w_main_ref, _OFF_Q, GLA_KEY)
    k_scr[...] = _proj(h, w_main_ref, _OFF_K, GLA_KEY)
    v_scr[...] = _proj(h, w_main_ref, _OFF_V, GLA_VAL)
    r_scr[...] = _proj(h, w_main_ref, _OFF_R, GLA_VAL)
    vn = _layernorm(_proj(h, w_main_ref, _OFF_GV, GMLP_WIDTH), lnw_ref[...], lnb_ref[...])
    vn_ref[...] = vn.reshape(nb, ns, GMLP_WIDTH)

    tril_mask = _tril_ones(ns)
    tril_bf16 = tril_mask.astype(BF16)
    ws = _masked_ws(ws_ref[...], ns).astype(BF16)
    bs_cols = bs_ref[...]
    gla_nw = gla_nw_ref[...]

    def stream(b, carry):
        rs = pl.ds(pl.multiple_of(b * ns, ns), ns)
        states = [s0_ref[b, hd] for hd in range(GLA_HEADS)]
        o_heads, states = _gla_chunk(q_scr[rs, :], k_scr[rs, :], v_scr[rs, :], la_scr[rs, :],
                                     states, tril_mask, tril_bf16)
        for hd in range(GLA_HEADS):
            state_ref[b, hd] = states[hd]
        gated_scr[rs, :] = _head_rms_gate(o_heads, r_scr[rs, :], gla_nw)
        s_scr[rs, :] = _spatial_gate(vn_ref[b], ws, bs_cols)
        return carry

    lax.fori_loop(0, nb, stream, 0)

    y_a = _mm(gated_scr[...], w_pa_ref[...])
    u = _proj(h, w_main_ref, _OFF_U, GMLP_WIDTH)
    z = _proj(h, w_main_ref, _OFF_Z, GMLP_WIDTH)
    y_b = _mm(u * s_scr[...] * _silu(z), w_pb_ref[...])
    g_a = _proj(h, w_main_ref, _OFF_GA, D_MODEL)
    g_b = _proj(h, w_main_ref, _OFF_GB, D_MODEL)
    merged = (jax.nn.sigmoid(g_a + bgate_ref[0:1, :]) * y_a
              + jax.nn.sigmoid(g_b + bgate_ref[1:2, :]) * y_b)
    upd = _mm(merged, w_out_ref[...]).reshape(nb, ns, D_MODEL)
    y_ref[...] = _rms(x3 + gate * upd, final_w_ref[...])


def _sample_call(x, ada, s0, small, weights):
    nb, ns, _ = x.shape
    rows = nb * ns
    args = (x, ada, s0, *small, *weights)
    full = lambda a: pl.BlockSpec(a.shape, lambda i, nd=a.ndim: (0,) * nd)
    return pl.pallas_call(
        _sample_kernel,
        grid=(1,),
        in_specs=[full(a) for a in args],
        out_specs=[pl.BlockSpec((nb, ns, D_MODEL), lambda i: (0, 0, 0)),
                   pl.BlockSpec(s0.shape, lambda i: (0, 0, 0, 0)),
                   pl.BlockSpec((nb, ns, GMLP_WIDTH), lambda i: (0, 0, 0))],
        out_shape=[jax.ShapeDtypeStruct((nb, ns, D_MODEL), F32),
                   jax.ShapeDtypeStruct(s0.shape, F32),
                   jax.ShapeDtypeStruct((nb, ns, GMLP_WIDTH), F32)],
        scratch_shapes=[pltpu.VMEM((rows, GLA_KEY), F32), pltpu.VMEM((rows, GLA_KEY), F32),
                        pltpu.VMEM((rows, GLA_VAL), F32), pltpu.VMEM((rows, GLA_VAL), F32),
                        pltpu.VMEM((rows, GLA_KEY), F32), pltpu.VMEM((rows, GLA_VAL), F32),
                        pltpu.VMEM((rows, GMLP_WIDTH), F32)],
        compiler_params=pltpu.CompilerParams(dimension_semantics=("arbitrary",),
                                             vmem_limit_bytes=VMEM_LIMIT_BYTES),
        name="sample_layer",
    )(*args)


def kernel(x_prompt, x_sample, state_gla, c_prompt, c_sample, norm_w, w_ada, b_ada, w_in, w_a2,
           b_a, gla_norm_w, ln_v_w, ln_v_b, w_s, b_s, b_gate, w_proj_a, w_proj_b, w_out,
           final_norm_w):
    depth = norm_w.shape[0]
    assert depth == 1 and x_prompt.shape[0] == 1
    nb, ns, _ = x_sample.shape
    lyr = 0

    c_all = jnp.concatenate([c_sample, c_prompt], axis=0)
    pad = (-c_all.shape[0]) % 8
    c_all = jnp.pad(c_all, ((0, pad), (0, 0)))
    ada = _ada_call(c_all, w_ada[lyr], b_ada[lyr][None, :])
    ada_sample = ada[:nb][:, None, :]
    ada_prompt = ada[nb:nb + 1]

    a0 = 2 * GLA_KEY + 2 * GLA_VAL
    w_t = jnp.transpose(w_in[lyr])
    w_main = _repack_call(w_t, a0, GLA_LOWRANK)
    w_alr = jnp.pad(jnp.transpose(w_t[a0:a0 + GLA_LOWRANK]),
                    ((0, 0), (0, LANES - GLA_LOWRANK))).astype(BF16)
    w_a2p = jnp.pad(w_a2[lyr], ((0, LANES - GLA_LOWRANK), (0, 0))).astype(BF16)
    weights = (w_main, w_alr, w_a2p, w_proj_a[lyr].astype(BF16), w_proj_b[lyr].astype(BF16),
               w_out[lyr].astype(BF16))

    def small(chunk):
        return (norm_w[lyr][None, :], b_a[lyr][None, :], gla_norm_w[lyr][None, :],
                ln_v_w[lyr][None, :], ln_v_b[lyr][None, :], w_s[lyr][:, :chunk, :chunk],
                b_s[lyr][:, :chunk].T, b_gate[lyr], final_norm_w[None, :])

    y_p, st_p = _prompt_call(x_prompt[0], ada_prompt, small(GMLP_CHUNK), weights)
    y_s, st_s, vn_s = _sample_call(x_sample, ada_sample, state_gla[lyr], small(min(GMLP_CHUNK, ns)),
                                   weights)
    return (y_p[None], y_s, st_p[None, None], st_s[None], vn_s[None])
```

```python
import functools

import jax
import jax.numpy as jnp
from jax import lax
from jax.experimental import pallas as pl
from jax.experimental.pallas import tpu as pltpu

D_MODEL = 1024
GLA_HEADS = 4
GLA_DK = 128
GLA_DV = 256
GLA_KEY = GLA_HEADS * GLA_DK
GLA_VAL = GLA_HEADS * GLA_DV
GLA_LOWRANK = 16
GLA_TAU = 16.0
GLA_CHUNK = 64
GMLP_CHUNK = 128
GMLP_GROUPS = 4
GMLP_WIDTH = 1024
GMLP_DG = GMLP_WIDTH // GMLP_GROUPS
EPS = 1e-6

LANES = 128
PROMPT_TILE = 256
ADA_COLS = 512
REPACK_ROWS = 512
VMEM_LIMIT_BYTES = 56 * 1024 * 1024

_OFF_Q = 0
_OFF_K = _OFF_Q + GLA_KEY
_OFF_V = _OFF_K + GLA_KEY
_OFF_R = _OFF_V + GLA_VAL
_OFF_U = _OFF_R + GLA_VAL
_OFF_GV = _OFF_U + GMLP_WIDTH
_OFF_Z = _OFF_GV + GMLP_WIDTH
_OFF_GA = _OFF_Z + GMLP_WIDTH
_OFF_GB = _OFF_GA + D_MODEL
MAIN_COLS = _OFF_GB + D_MODEL

BF16 = jnp.bfloat16
F32 = jnp.float32


def _mm(a, b):
    return jnp.dot(a.astype(BF16), b.astype(BF16), preferred_element_type=F32)


def _mm_nt(a, b):
    return lax.dot_general(a.astype(BF16), b.astype(BF16), (((1,), (1,)), ((), ())),
                           preferred_element_type=F32)


def _mm_tn(a, b):
    return lax.dot_general(a.astype(BF16), b.astype(BF16), (((0,), (0,)), ((), ())),
                           preferred_element_type=F32)


def _silu(x):
    return x * jax.nn.sigmoid(x)


def _log_sigmoid(x):
    return jnp.minimum(x, 0.0) - jnp.log1p(jnp.exp(-jnp.abs(x)))


def _rms(x, w):
    return x * lax.rsqrt(jnp.mean(x * x, axis=-1, keepdims=True) + EPS) * w


def _tril_ones(n):
    r = lax.broadcasted_iota(jnp.int32, (n, n), 0)
    c = lax.broadcasted_iota(jnp.int32, (n, n), 1)
    return r >= c


def _cumsum_rows(la, tril_bf16):
    hi = la.astype(BF16)
    lo = (la - hi.astype(F32)).astype(BF16)
    return (jnp.dot(tril_bf16, hi, preferred_element_type=F32)
            + jnp.dot(tril_bf16, lo, preferred_element_type=F32))


def _gla_decay_operands(q, k, b):
    c = q.shape[0]
    mid = b[c // 2:c // 2 + 1, :]
    last = b[c - 1:c, :]
    qd = q * (GLA_DK ** -0.5) * jnp.exp(b - mid)
    kd = k * jnp.exp(mid - b)
    qs = qd * jnp.exp(mid)
    kl = kd * jnp.exp(last - mid)
    return qd.astype(BF16), kd.astype(BF16), qs.astype(BF16), kl.astype(BF16), jnp.exp(last)


def _head(x, h, width):
    return x[:, h * width:(h + 1) * width]


def _gla_scores(qd, kd, tril_mask):
    return [jnp.where(tril_mask, _mm_nt(_head(qd, h, GLA_DK), _head(kd, h, GLA_DK)), 0.0)
            .astype(BF16) for h in range(GLA_HEADS)]


def _gla_state_step(kl, v, e_last):
    steps = []
    for h in range(GLA_HEADS):
        dec = jnp.broadcast_to(_head(e_last, h, GLA_DK), (GLA_DK, GLA_DK)).T
        dec = jnp.concatenate([dec] * (GLA_DV // GLA_DK), axis=1)
        steps.append((dec, _mm_tn(_head(kl, h, GLA_DK), _head(v, h, GLA_DV))))
    return steps


def _gla_outputs(qs, scores, v, states):
    outs = []
    for h in range(GLA_HEADS):
        lhs = jnp.concatenate([_head(qs, h, GLA_DK), scores[h]], axis=1)
        rhs = jnp.concatenate([states[h].astype(BF16), _head(v, h, GLA_DV)], axis=0)
        outs.append(jnp.dot(lhs, rhs, preferred_element_type=F32))
    return outs


def _gla_chunk(q, k, v, la, states, tril_mask, tril_bf16):
    v = v.astype(BF16)
    qd, kd, qs, kl, e_last = _gla_decay_operands(q, k, _cumsum_rows(la, tril_bf16))
    outs = _gla_outputs(qs, _gla_scores(qd, kd, tril_mask), v, states)
    steps = _gla_state_step(kl, v, e_last)
    return outs, [states[h] * steps[h][0] + steps[h][1] for h in range(GLA_HEADS)]


def _head_rms_gate(o_heads, r, gla_norm_w):
    cols = []
    for h in range(GLA_HEADS):
        vs = slice(h * GLA_DV, (h + 1) * GLA_DV)
        cols.append(_rms(o_heads[h], gla_norm_w) * _silu(r[:, vs]))
    return jnp.concatenate(cols, axis=1)


def _layernorm(x, w, b):
    mu = jnp.mean(x, axis=-1, keepdims=True)
    xc = x - mu
    var = jnp.mean(xc * xc, axis=-1, keepdims=True)
    return xc * lax.rsqrt(var + EPS) * w + b


def _spatial_gate(vn, ws_bf16, bs_cols):
    cols = []
    for g in range(GMLP_GROUPS):
        gs = slice(g * GMLP_DG, (g + 1) * GMLP_DG)
        cols.append(_mm(ws_bf16[g], vn[:, gs]) + bs_cols[:, g:g + 1])
    return jnp.concatenate(cols, axis=1)


def _masked_ws(ws, c):
    r = lax.broadcasted_iota(jnp.int32, (c, c), 0) // GLA_CHUNK
    s = lax.broadcasted_iota(jnp.int32, (c, c), 1) // GLA_CHUNK
    return jnp.where((r >= s)[None], ws, 0.0)


def _ada_kernel(c_ref, w_ref, b_ref, o_ref):
    o_ref[...] = _mm(_silu(c_ref[...]), w_ref[...]) + b_ref[...]


def _ada_call(c_all, w_ada, b_ada):
    rows = c_all.shape[0]
    return pl.pallas_call(
        _ada_kernel,
        grid=(3 * D_MODEL // ADA_COLS,),
        in_specs=[pl.BlockSpec((rows, D_MODEL), lambda j: (0, 0)),
                  pl.BlockSpec((D_MODEL, ADA_COLS), lambda j: (0, j)),
                  pl.BlockSpec((1, ADA_COLS), lambda j: (0, j))],
        out_specs=pl.BlockSpec((rows, ADA_COLS), lambda j: (0, j)),
        out_shape=jax.ShapeDtypeStruct((rows, 3 * D_MODEL), F32),
        name="ada_proj",
    )(c_all, w_ada, b_ada)


def _repack_kernel(wt_ref, gap_ref, o_ref, gap_o_ref, *, gap):
    o_ref[...] = wt_ref[...].T.astype(BF16)

    @pl.when(pl.program_id(0) == 0)
    def _gap_columns():
        cols = gap_ref[...].T
        lane = lax.broadcasted_iota(jnp.int32, cols.shape, 1)
        gap_o_ref[...] = jnp.where(lane < gap, cols, 0.0).astype(BF16)


def _repack_call(w_t, gap_start, gap):
    rows = REPACK_ROWS
    assert gap_start % rows == 0 and MAIN_COLS % rows == 0
    first_after_gap = gap_start // rows

    def src_row(j):
        return (j * (rows // gap) + jnp.where(j < first_after_gap, 0, 1)) * gap

    return pl.pallas_call(
        functools.partial(_repack_kernel, gap=gap),
        grid=(MAIN_COLS // rows,),
        in_specs=[pl.BlockSpec((pl.Element(rows), pl.Element(D_MODEL)),
                               lambda j: (src_row(j), 0)),
                  pl.BlockSpec((pl.Element(LANES), pl.Element(D_MODEL)),
                               lambda j: (gap_start, 0))],
        out_specs=[pl.BlockSpec((D_MODEL, rows), lambda j: (0, j)),
                   pl.BlockSpec((D_MODEL, LANES), lambda j: (0, 0))],
        out_shape=[jax.ShapeDtypeStruct((D_MODEL, MAIN_COLS), BF16),
                   jax.ShapeDtypeStruct((D_MODEL, LANES), BF16)],
        name="repack_w_in",
    )(w_t, w_t)


def _modulated_input(x, shift, scale, norm_w):
    return (_rms(x, norm_w) * (1.0 + scale) + shift).astype(BF16)


def _proj(h, w_ref, off, width):
    return jnp.dot(h, w_ref[:, off:off + width], preferred_element_type=F32)


def _log_decay(h, w_alr_ref, w_a2_ref, b_a):
    a_lr = jnp.dot(h, w_alr_ref[...], preferred_element_type=F32)
    return _log_sigmoid(_mm(a_lr, w_a2_ref[...]) + b_a) * (1.0 / GLA_TAU)


def _prompt_kernel(x_ref, xn_ref, ada_ref, norm_w_ref, b_a_ref, gla_nw_ref, lnw_ref, lnb_ref,
                   ws_ref, bs_ref, bgate_ref, final_w_ref, w_main_ref, w_alr_ref, w_a2_ref,
                   w_pa_ref, w_pb_ref, w_out_ref, y_ref, state_ref, st_scr, ws_scr, h_scr,
                   out_scr):
    i = pl.program_id(0)
    n_tiles = pl.num_programs(0) - 1
    slot = lax.rem(i, 2)

    def next_input(src_ref, dst_slot):
        h_scr[dst_slot] = _modulated_input(src_ref[...], ada_ref[:, 0:D_MODEL],
                                           ada_ref[:, D_MODEL:2 * D_MODEL], norm_w_ref[...])

    def previous_output():
        y_ref[...] = _rms(out_scr[...], final_w_ref[...])

    @pl.when(i == 0)
    def _fill():
        st_scr[...] = jnp.zeros_like(st_scr)
        out_scr[...] = jnp.zeros_like(out_scr)
        ws_scr[...] = _masked_ws(ws_ref[...], GMLP_CHUNK).astype(BF16)
        next_input(x_ref, 0)

    @pl.when(i < n_tiles)
    def _steady():
        _prompt_tile_body(x_ref, ada_ref, b_a_ref, gla_nw_ref, lnw_ref, lnb_ref, bs_ref,
                          bgate_ref, w_main_ref, w_alr_ref, w_a2_ref, w_pa_ref, w_pb_ref,
                          w_out_ref, st_scr, ws_scr, h_scr.at[slot], out_scr,
                          previous_output, functools.partial(next_input, xn_ref, 1 - slot))

    @pl.when(i == n_tiles)
    def _drain():
        previous_output()
        state_ref[...] = st_scr[...]


def _prompt_tile_body(x_ref, ada_ref, b_a_ref, gla_nw_ref, lnw_ref, lnb_ref, bs_ref, bgate_ref,
                      w_main_ref, w_alr_ref, w_a2_ref, w_pa_ref, w_pb_ref, w_out_ref, st_scr,
                      ws_scr, h_ref, out_scr, previous_output, next_input):
    tile = x_ref.shape[0]
    gate = ada_ref[:, 2 * D_MODEL:3 * D_MODEL]
    h = h_ref[...]

    chunks = [slice(c * GLA_CHUNK, (c + 1) * GLA_CHUNK) for c in range(tile // GLA_CHUNK)]
    tril_mask = _tril_ones(GLA_CHUNK)
    tril_bf16 = tril_mask.astype(BF16)
    la = _log_decay(h, w_alr_ref, w_a2_ref, b_a_ref[...])
    q = _proj(h, w_main_ref, _OFF_Q, GLA_KEY)
    b = [_cumsum_rows(la[rows], tril_bf16) for rows in chunks]
    k = _proj(h, w_main_ref, _OFF_K, GLA_KEY)
    previous_output()
    v = _proj(h, w_main_ref, _OFF_V, GLA_VAL).astype(BF16)
    ops = [_gla_decay_operands(q[rows], k[rows], b[c]) for c, rows in enumerate(chunks)]
    r = _proj(h, w_main_ref, _OFF_R, GLA_VAL)
    scores = [_gla_scores(ops[c][0], ops[c][1], tril_mask) for c in range(len(chunks))]
    steps = [_gla_state_step(ops[c][3], v[rows], ops[c][4]) for c, rows in enumerate(chunks)]
    gv = _proj(h, w_main_ref, _OFF_GV, GMLP_WIDTH)
    states = [[st_scr[hd] for hd in range(GLA_HEADS)]]
    for c in range(len(chunks)):
        states.append([states[c][hd] * steps[c][hd][0] + steps[c][hd][1]
                       for hd in range(GLA_HEADS)])
    for hd in range(GLA_HEADS):
        st_scr[hd] = states[-1][hd]
    o = [_gla_outputs(ops[c][2], scores[c], v[rows], states[c]) for c, rows in enumerate(chunks)]
    u = _proj(h, w_main_ref, _OFF_U, GMLP_WIDTH)

    vn = _layernorm(gv, lnw_ref[...], lnb_ref[...])
    ws = ws_scr[...]
    s = jnp.concatenate(
        [_spatial_gate(vn[j * GMLP_CHUNK:(j + 1) * GMLP_CHUNK], ws, bs_ref[...])
         for j in range(tile // GMLP_CHUNK)], axis=0)
    z = _proj(h, w_main_ref, _OFF_Z, GMLP_WIDTH)
    gated = jnp.concatenate(
        [_head_rms_gate(o[c], r[rows], gla_nw_ref[...]) for c, rows in enumerate(chunks)], axis=0)
    g_a = _proj(h, w_main_ref, _OFF_GA, D_MODEL)
    g_b = _proj(h, w_main_ref, _OFF_GB, D_MODEL)
    y_a = _mm(gated, w_pa_ref[...])
    next_input()
    y_b = _mm(u * s * _silu(z), w_pb_ref[...])

    merged = (jax.nn.sigmoid(g_a + bgate_ref[0:1, :]) * y_a
              + jax.nn.sigmoid(g_b + bgate_ref[1:2, :]) * y_b)
    out_scr[...] = x_ref[...] + gate * _mm(merged, w_out_ref[...])


def _const_spec(shape):
    nd = len(shape)
    return pl.BlockSpec(shape, lambda i: (0,) * nd, pipeline_mode=pl.Buffered(1))


def _prompt_call(x, ada, small, weights):
    seq = x.shape[0]
    tile = PROMPT_TILE
    n_tiles = seq // tile
    last = n_tiles - 1
    in_specs = ([pl.BlockSpec((tile, D_MODEL), lambda i: (jnp.minimum(i, last), 0)),
                 pl.BlockSpec((tile, D_MODEL), lambda i: (jnp.minimum(i + 1, last), 0)),
                 _const_spec(ada.shape)]
                + [_const_spec(a.shape) for a in small]
                + [_const_spec(w.shape) for w in weights])
    return pl.pallas_call(
        _prompt_kernel,
        grid=(n_tiles + 1,),
        in_specs=in_specs,
        out_specs=[pl.BlockSpec((tile, D_MODEL), lambda i: (jnp.maximum(i - 1, 0), 0)),
                   pl.BlockSpec((GLA_HEADS, GLA_DK, GLA_DV), lambda i: (0, 0, 0))],
        out_shape=[jax.ShapeDtypeStruct((seq, D_MODEL), F32),
                   jax.ShapeDtypeStruct((GLA_HEADS, GLA_DK, GLA_DV), F32)],
        scratch_shapes=[pltpu.VMEM((GLA_HEADS, GLA_DK, GLA_DV), F32),
                        pltpu.VMEM((GMLP_GROUPS, GMLP_CHUNK, GMLP_CHUNK), BF16),
                        pltpu.VMEM((2, tile, D_MODEL), BF16),
                        pltpu.VMEM((tile, D_MODEL), F32)],
        compiler_params=pltpu.CompilerParams(dimension_semantics=("arbitrary",),
                                             vmem_limit_bytes=VMEM_LIMIT_BYTES),
        name="prompt_layer",
    )(x, x, ada, *small, *weights)


def _sample_kernel(x_ref, ada_ref, s0_ref, norm_w_ref, b_a_ref, gla_nw_ref, lnw_ref, lnb_ref,
                   ws_ref, bs_ref, bgate_ref, final_w_ref, w_main_ref, w_alr_ref, w_a2_ref,
                   w_pa_ref, w_pb_ref, w_out_ref, y_ref, state_ref, vn_ref,
                   q_scr, k_scr, v_scr, r_scr, la_scr, gated_scr, s_scr):
    nb, ns, _ = x_ref.shape
    rows = nb * ns
    x3 = x_ref[...]
    shift = ada_ref[:, :, 0:D_MODEL]
    scale = ada_ref[:, :, D_MODEL:2 * D_MODEL]
    gate = ada_ref[:, :, 2 * D_MODEL:3 * D_MODEL]
    h = _modulated_input(x3, shift, scale, norm_w_ref[...]).reshape(rows, D_MODEL)

    la_scr[...] = _log_decay(h, w_alr_ref, w_a2_ref, b_a_ref[...])
    q_scr[...] = _proj(h, w_main_ref, _OFF_Q, GLA_KEY)
    k_scr[...] = _proj(h, w_main_ref, _OFF_K, GLA_KEY)
    v_scr[...] = _proj(h, w_main_ref, _OFF_V, GLA_VAL)
    r_scr[...] = _proj(h, w_main_ref, _OFF_R, GLA_VAL)
    vn = _layernorm(_proj(h, w_main_ref, _OFF_GV, GMLP_WIDTH), lnw_ref[...], lnb_ref[...])
    vn_ref[...] = vn.reshape(nb, ns, GMLP_WIDTH)

    tril_mask = _tril_ones(ns)
    tril_bf16 = tril_mask.astype(BF16)
    ws = _masked_ws(ws_ref[...], ns).astype(BF16)
    bs_cols = bs_ref[...]
    gla_nw = gla_nw_ref[...]

    def stream(b, carry):
        rs = pl.ds(pl.multiple_of(b * ns, ns), ns)
        states = [s0_ref[b, hd] for hd in range(GLA_HEADS)]
        o_heads, states = _gla_chunk(q_scr[rs, :], k_scr[rs, :], v_scr[rs, :], la_scr[rs, :],
                                     states, tril_mask, tril_bf16)
        for hd in range(GLA_HEADS):
            state_ref[b, hd] = states[hd]
        gated_scr[rs, :] = _head_rms_gate(o_heads, r_scr[rs, :], gla_nw)
        s_scr[rs, :] = _spatial_gate(vn_ref[b], ws, bs_cols)
        return carry

    lax.fori_loop(0, nb, stream, 0)

    y_a = _mm(gated_scr[...], w_pa_ref[...])
    u = _proj(h, w_main_ref, _OFF_U, GMLP_WIDTH)
    z = _proj(h, w_main_ref, _OFF_Z, GMLP_WIDTH)
    y_b = _mm(u * s_scr[...] * _silu(z), w_pb_ref[...])
    g_a = _proj(h, w_main_ref, _OFF_GA, D_MODEL)
    g_b = _proj(h, w_main_ref, _OFF_GB, D_MODEL)
    merged = (jax.nn.sigmoid(g_a + bgate_ref[0:1, :]) * y_a
              + jax.nn.sigmoid(g_b + bgate_ref[1:2, :]) * y_b)
    upd = _mm(merged, w_out_ref[...]).reshape(nb, ns, D_MODEL)
    y_ref[...] = _rms(x3 + gate * upd, final_w_ref[...])


def _sample_call(x, ada, s0, small, weights):
    nb, ns, _ = x.shape
    rows = nb * ns
    args = (x, ada, s0, *small, *weights)
    full = lambda a: pl.BlockSpec(a.shape, lambda i, nd=a.ndim: (0,) * nd)
    return pl.pallas_call(
        _sample_kernel,
        grid=(1,),
        in_specs=[full(a) for a in args],
        out_specs=[pl.BlockSpec((nb, ns, D_MODEL), lambda i: (0, 0, 0)),
                   pl.BlockSpec(s0.shape, lambda i: (0, 0, 0, 0)),
                   pl.BlockSpec((nb, ns, GMLP_WIDTH), lambda i: (0, 0, 0))],
        out_shape=[jax.ShapeDtypeStruct((nb, ns, D_MODEL), F32),
                   jax.ShapeDtypeStruct(s0.shape, F32),
                   jax.ShapeDtypeStruct((nb, ns, GMLP_WIDTH), F32)],
        scratch_shapes=[pltpu.VMEM((rows, GLA_KEY), F32), pltpu.VMEM((rows, GLA_KEY), F32),
                        pltpu.VMEM((rows, GLA_VAL), F32), pltpu.VMEM((rows, GLA_VAL), F32),
                        pltpu.VMEM((rows, GLA_KEY), F32), pltpu.VMEM((rows, GLA_VAL), F32),
                        pltpu.VMEM((rows, GMLP_WIDTH), F32)],
        compiler_params=pltpu.CompilerParams(dimension_semantics=("arbitrary",),
                                             vmem_limit_bytes=VMEM_LIMIT_BYTES),
        name="sample_layer",
    )(*args)


def kernel(x_prompt, x_sample, state_gla, c_prompt, c_sample, norm_w, w_ada, b_ada, w_in, w_a2,
           b_a, gla_norm_w, ln_v_w, ln_v_b, w_s, b_s, b_gate, w_proj_a, w_proj_b, w_out,
           final_norm_w):
    depth = norm_w.shape[0]
    assert depth == 1 and x_prompt.shape[0] == 1
    nb, ns, _ = x_sample.shape
    lyr = 0

    c_all = jnp.concatenate([c_sample, c_prompt], axis=0)
    pad = (-c_all.shape[0]) % 8
    c_all = jnp.pad(c_all, ((0, pad), (0, 0)))
    ada = _ada_call(c_all, w_ada[lyr], b_ada[lyr][None, :])
    ada_sample = ada[:nb][:, None, :]
    ada_prompt = ada[nb:nb + 1]

    a0 = 2 * GLA_KEY + 2 * GLA_VAL
    w_t = jnp.transpose(w_in[lyr])
    w_main, w_alr = _repack_call(w_t, a0, GLA_LOWRANK)
    w_a2p = jnp.pad(w_a2[lyr], ((0, LANES - GLA_LOWRANK), (0, 0))).astype(BF16)
    weights = (w_main, w_alr, w_a2p, w_proj_a[lyr].astype(BF16), w_proj_b[lyr].astype(BF16),
               w_out[lyr].astype(BF16))

    def small(chunk):
        return (norm_w[lyr][None, :], b_a[lyr][None, :], gla_norm_w[lyr][None, :],
                ln_v_w[lyr][None, :], ln_v_b[lyr][None, :], w_s[lyr][:, :chunk, :chunk],
                b_s[lyr][:, :chunk].T, b_gate[lyr], final_norm_w[None, :])

    y_p, st_p = _prompt_call(x_prompt[0], ada_prompt, small(GMLP_CHUNK), weights)
    y_s, st_s, vn_s = _sample_call(x_sample, ada_sample, state_gla[lyr], small(min(GMLP_CHUNK, ns)),
                                   weights)
    return (y_p[None], y_s, st_p[None, None], st_s[None], vn_s[None])
```

```python
import functools

import jax
import jax.numpy as jnp
from jax import lax
from jax.experimental import pallas as pl
from jax.experimental.pallas import tpu as pltpu

D_MODEL = 1024
GLA_HEADS = 4
GLA_DK = 128
GLA_DV = 256
GLA_KEY = GLA_HEADS * GLA_DK
GLA_VAL = GLA_HEADS * GLA_DV
GLA_LOWRANK = 16
GLA_TAU = 16.0
GLA_CHUNK = 64
GMLP_CHUNK = 128
GMLP_GROUPS = 4
GMLP_WIDTH = 1024
GMLP_DG = GMLP_WIDTH // GMLP_GROUPS
EPS = 1e-6

LANES = 128
PROMPT_TILE = 256
ADA_COLS = 1024
REPACK_ROWS = 512
VMEM_LIMIT_BYTES = 56 * 1024 * 1024

_OFF_Q = 0
_OFF_K = _OFF_Q + GLA_KEY
_OFF_V = _OFF_K + GLA_KEY
_OFF_R = _OFF_V + GLA_VAL
_OFF_U = _OFF_R + GLA_VAL
_OFF_GV = _OFF_U + GMLP_WIDTH
_OFF_Z = _OFF_GV + GMLP_WIDTH
_OFF_GA = _OFF_Z + GMLP_WIDTH
_OFF_GB = _OFF_GA + D_MODEL
MAIN_COLS = _OFF_GB + D_MODEL

BF16 = jnp.bfloat16
F32 = jnp.float32


def _mm(a, b):
    return jnp.dot(a.astype(BF16), b.astype(BF16), preferred_element_type=F32)


def _mm_nt(a, b):
    return lax.dot_general(a.astype(BF16), b.astype(BF16), (((1,), (1,)), ((), ())),
                           preferred_element_type=F32)


def _mm_tn(a, b):
    return lax.dot_general(a.astype(BF16), b.astype(BF16), (((0,), (0,)), ((), ())),
                           preferred_element_type=F32)


def _silu(x):
    return x * jax.nn.sigmoid(x)


def _log_sigmoid(x):
    return jnp.minimum(x, 0.0) - jnp.log1p(jnp.exp(-jnp.abs(x)))


def _rms(x, w):
    return x * lax.rsqrt(jnp.mean(x * x, axis=-1, keepdims=True) + EPS) * w


def _tril_ones(n):
    r = lax.broadcasted_iota(jnp.int32, (n, n), 0)
    c = lax.broadcasted_iota(jnp.int32, (n, n), 1)
    return r >= c


def _cumsum_rows(la, tril_bf16):
    hi = la.astype(BF16)
    lo = (la - hi.astype(F32)).astype(BF16)
    return (jnp.dot(tril_bf16, hi, preferred_element_type=F32)
            + jnp.dot(tril_bf16, lo, preferred_element_type=F32))


def _segment_row(b, seg, offset):
    n = b.shape[0] // seg
    rows = [b[s * seg + offset:s * seg + offset + 1, :] for s in range(n)]
    if n == 1:
        return rows[0]
    return jnp.concatenate([jnp.broadcast_to(row, (seg, b.shape[1])) for row in rows], axis=0)


def _gla_decay_operands(q, k, b, seg):
    mid = _segment_row(b, seg, seg // 2)
    last = _segment_row(b, seg, seg - 1)
    qd = q * (GLA_DK ** -0.5) * jnp.exp(b - mid)
    kd = k * jnp.exp(mid - b)
    qs = qd * jnp.exp(mid)
    kl = kd * jnp.exp(last - mid)
    e_last = [jnp.exp(b[s * seg + seg - 1:(s + 1) * seg, :]) for s in range(b.shape[0] // seg)]
    return qd.astype(BF16), kd.astype(BF16), qs.astype(BF16), kl.astype(BF16), e_last


def _head(x, h, width):
    return x[:, h * width:(h + 1) * width]


def _gla_scores(qd, kd, tril_mask):
    return [jnp.where(tril_mask, _mm_nt(_head(qd, h, GLA_DK), _head(kd, h, GLA_DK)), 0.0)
            .astype(BF16) for h in range(GLA_HEADS)]


def _gla_state_step(kl, v, e_last):
    steps = []
    for h in range(GLA_HEADS):
        dec = jnp.broadcast_to(_head(e_last, h, GLA_DK), (GLA_DK, GLA_DK)).T
        dec = jnp.concatenate([dec] * (GLA_DV // GLA_DK), axis=1)
        steps.append((dec, _mm_tn(_head(kl, h, GLA_DK), _head(v, h, GLA_DV))))
    return steps


def _gla_outputs(qs, scores, v, states):
    outs = []
    for h in range(GLA_HEADS):
        lhs = jnp.concatenate([_head(qs, h, GLA_DK), scores[h]], axis=1)
        rhs = jnp.concatenate([states[h].astype(BF16), _head(v, h, GLA_DV)], axis=0)
        outs.append(jnp.dot(lhs, rhs, preferred_element_type=F32))
    return outs


def _head_rms_gate(o_heads, r, gla_norm_w):
    cols = []
    for h in range(GLA_HEADS):
        vs = slice(h * GLA_DV, (h + 1) * GLA_DV)
        cols.append(_rms(o_heads[h], gla_norm_w) * _silu(r[:, vs]))
    return jnp.concatenate(cols, axis=1)


def _layernorm(x, w, b):
    mu = jnp.mean(x, axis=-1, keepdims=True)
    xc = x - mu
    var = jnp.mean(xc * xc, axis=-1, keepdims=True)
    return xc * lax.rsqrt(var + EPS) * w + b


def _spatial_gate(vn, ws_bf16, bs_cols):
    cols = []
    for g in range(GMLP_GROUPS):
        gs = slice(g * GMLP_DG, (g + 1) * GMLP_DG)
        cols.append(_mm(ws_bf16[g], vn[:, gs]) + bs_cols[:, g:g + 1])
    return jnp.concatenate(cols, axis=1)


def _masked_ws(ws, c):
    r = lax.broadcasted_iota(jnp.int32, (c, c), 0) // GLA_CHUNK
    s = lax.broadcasted_iota(jnp.int32, (c, c), 1) // GLA_CHUNK
    return jnp.where((r >= s)[None], ws, 0.0)


def _ada_kernel(c_ref, w_ref, b_ref, o_ref):
    o_ref[...] = _mm(_silu(c_ref[...]), w_ref[...]) + b_ref[...]


def _ada_call(c_all, w_ada, b_ada):
    rows = c_all.shape[0]
    return pl.pallas_call(
        _ada_kernel,
        grid=(3 * D_MODEL // ADA_COLS,),
        in_specs=[pl.BlockSpec((rows, D_MODEL), lambda j: (0, 0)),
                  pl.BlockSpec((D_MODEL, ADA_COLS), lambda j: (0, j)),
                  pl.BlockSpec((1, ADA_COLS), lambda j: (0, j))],
        out_specs=pl.BlockSpec((rows, ADA_COLS), lambda j: (0, j)),
        out_shape=jax.ShapeDtypeStruct((rows, 3 * D_MODEL), F32),
        name="ada_proj",
    )(c_all, w_ada, b_ada)


def _repack_kernel(wt_ref, gap_ref, o_ref, gap_o_ref, *, gap):
    o_ref[...] = wt_ref[...].T.astype(BF16)

    @pl.when(pl.program_id(0) == 0)
    def _gap_columns():
        cols = gap_ref[...].T
        lane = lax.broadcasted_iota(jnp.int32, cols.shape, 1)
        gap_o_ref[...] = jnp.where(lane < gap, cols, 0.0).astype(BF16)


def _repack_call(w_t, gap_start, gap):
    rows = REPACK_ROWS
    assert gap_start % rows == 0 and MAIN_COLS % rows == 0
    first_after_gap = gap_start // rows

    def src_row(j):
        return (j * (rows // gap) + jnp.where(j < first_after_gap, 0, 1)) * gap

    return pl.pallas_call(
        functools.partial(_repack_kernel, gap=gap),
        grid=(MAIN_COLS // rows,),
        in_specs=[pl.BlockSpec((pl.Element(rows), pl.Element(D_MODEL)),
                               lambda j: (src_row(j), 0)),
                  pl.BlockSpec((pl.Element(LANES), pl.Element(D_MODEL)),
                               lambda j: (gap_start, 0))],
        out_specs=[pl.BlockSpec((D_MODEL, rows), lambda j: (0, j)),
                   pl.BlockSpec((D_MODEL, LANES), lambda j: (0, 0))],
        out_shape=[jax.ShapeDtypeStruct((D_MODEL, MAIN_COLS), BF16),
                   jax.ShapeDtypeStruct((D_MODEL, LANES), BF16)],
        name="repack_w_in",
    )(w_t, w_t)


def _modulated_input(x, shift, scale, norm_w):
    return (_rms(x, norm_w) * (1.0 + scale) + shift).astype(BF16)


def _proj(h, w_ref, off, width):
    return jnp.dot(h, w_ref[:, off:off + width], preferred_element_type=F32)


def _log_decay(h, w_alr_ref, w_a2_ref, b_a):
    a_lr = jnp.dot(h, w_alr_ref[...], preferred_element_type=F32)
    return _log_sigmoid(_mm(a_lr, w_a2_ref[...]) + b_a) * (1.0 / GLA_TAU)


def _prompt_kernel(x_ref, xn_ref, ada_ref, norm_w_ref, b_a_ref, gla_nw_ref, lnw_ref, lnb_ref,
                   ws_ref, bs_ref, bgate_ref, final_w_ref, w_main_ref, w_alr_ref, w_a2_ref,
                   w_pa_ref, w_pb_ref, w_out_ref, y_ref, state_ref, st_scr, ws_scr, h_scr,
                   out_scr):
    i = pl.program_id(0)
    n_tiles = pl.num_programs(0) - 1
    slot = lax.rem(i, 2)

    def next_input(src_ref, dst_slot):
        h_scr[dst_slot] = _modulated_input(src_ref[...], ada_ref[:, 0:D_MODEL],
                                           ada_ref[:, D_MODEL:2 * D_MODEL], norm_w_ref[...])

    def previous_output():
        y_ref[...] = _rms(out_scr[...], final_w_ref[...])

    @pl.when(i == 0)
    def _fill():
        st_scr[...] = jnp.zeros_like(st_scr)
        out_scr[...] = jnp.zeros_like(out_scr)
        ws_scr[...] = _masked_ws(ws_ref[...], GMLP_CHUNK).astype(BF16)
        next_input(x_ref, 0)

    @pl.when(i < n_tiles)
    def _steady():
        _prompt_tile_body(x_ref, ada_ref, b_a_ref, gla_nw_ref, lnw_ref, lnb_ref, bs_ref,
                          bgate_ref, w_main_ref, w_alr_ref, w_a2_ref, w_pa_ref, w_pb_ref,
                          w_out_ref, st_scr, ws_scr, h_scr.at[slot], out_scr,
                          previous_output, functools.partial(next_input, xn_ref, 1 - slot))

    @pl.when(i == n_tiles)
    def _drain():
        previous_output()
        state_ref[...] = st_scr[...]


def _prompt_tile_body(x_ref, ada_ref, b_a_ref, gla_nw_ref, lnw_ref, lnb_ref, bs_ref, bgate_ref,
                      w_main_ref, w_alr_ref, w_a2_ref, w_pa_ref, w_pb_ref, w_out_ref, st_scr,
                      ws_scr, h_ref, out_scr, previous_output, next_input):
    tile = x_ref.shape[0]
    gate = ada_ref[:, 2 * D_MODEL:3 * D_MODEL]
    h = h_ref[...]

    chunks = [slice(c * GLA_CHUNK, (c + 1) * GLA_CHUNK) for c in range(tile // GLA_CHUNK)]
    tril_mask = _tril_ones(GLA_CHUNK)
    tril_bf16 = tril_mask.astype(BF16)
    la = _log_decay(h, w_alr_ref, w_a2_ref, b_a_ref[...])
    q = _proj(h, w_main_ref, _OFF_Q, GLA_KEY)
    b = [_cumsum_rows(la[rows], tril_bf16) for rows in chunks]
    k = _proj(h, w_main_ref, _OFF_K, GLA_KEY)
    previous_output()
    v = _proj(h, w_main_ref, _OFF_V, GLA_VAL).astype(BF16)
    ops = [_gla_decay_operands(q[rows], k[rows], b[c], GLA_CHUNK)
           for c, rows in enumerate(chunks)]
    r = _proj(h, w_main_ref, _OFF_R, GLA_VAL)
    scores = [_gla_scores(ops[c][0], ops[c][1], tril_mask) for c in range(len(chunks))]
    steps = [_gla_state_step(ops[c][3], v[rows], ops[c][4][0]) for c, rows in enumerate(chunks)]
    gv = _proj(h, w_main_ref, _OFF_GV, GMLP_WIDTH)
    states = [[st_scr[hd] for hd in range(GLA_HEADS)]]
    for c in range(len(chunks)):
        states.append([states[c][hd] * steps[c][hd][0] + steps[c][hd][1]
                       for hd in range(GLA_HEADS)])
    for hd in range(GLA_HEADS):
        st_scr[hd] = states[-1][hd]
    o = [_gla_outputs(ops[c][2], scores[c], v[rows], states[c]) for c, rows in enumerate(chunks)]
    u = _proj(h, w_main_ref, _OFF_U, GMLP_WIDTH)

    vn = _layernorm(gv, lnw_ref[...], lnb_ref[...])
    ws = ws_scr[...]
    s = jnp.concatenate(
        [_spatial_gate(vn[j * GMLP_CHUNK:(j + 1) * GMLP_CHUNK], ws, bs_ref[...])
         for j in range(tile // GMLP_CHUNK)], axis=0)
    z = _proj(h, w_main_ref, _OFF_Z, GMLP_WIDTH)
    gated = jnp.concatenate(
        [_head_rms_gate(o[c], r[rows], gla_nw_ref[...]) for c, rows in enumerate(chunks)], axis=0)
    g_a = _proj(h, w_main_ref, _OFF_GA, D_MODEL)
    g_b = _proj(h, w_main_ref, _OFF_GB, D_MODEL)
    y_a = _mm(gated, w_pa_ref[...])
    next_input()
    y_b = _mm(u * s * _silu(z), w_pb_ref[...])

    merged = (jax.nn.sigmoid(g_a + bgate_ref[0:1, :]) * y_a
              + jax.nn.sigmoid(g_b + bgate_ref[1:2, :]) * y_b)
    out_scr[...] = x_ref[...] + gate * _mm(merged, w_out_ref[...])


def _const_spec(shape):
    nd = len(shape)
    return pl.BlockSpec(shape, lambda i: (0,) * nd, pipeline_mode=pl.Buffered(1))


def _prompt_call(x, ada, small, weights):
    seq = x.shape[0]
    tile = PROMPT_TILE
    n_tiles = seq // tile
    last = n_tiles - 1
    in_specs = ([pl.BlockSpec((tile, D_MODEL), lambda i: (jnp.minimum(i, last), 0)),
                 pl.BlockSpec((tile, D_MODEL), lambda i: (jnp.minimum(i + 1, last), 0)),
                 _const_spec(ada.shape)]
                + [_const_spec(a.shape) for a in small]
                + [_const_spec(w.shape) for w in weights])
    return pl.pallas_call(
        _prompt_kernel,
        grid=(n_tiles + 1,),
        in_specs=in_specs,
        out_specs=[pl.BlockSpec((tile, D_MODEL), lambda i: (jnp.maximum(i - 1, 0), 0)),
                   pl.BlockSpec((GLA_HEADS, GLA_DK, GLA_DV), lambda i: (0, 0, 0))],
        out_shape=[jax.ShapeDtypeStruct((seq, D_MODEL), F32),
                   jax.ShapeDtypeStruct((GLA_HEADS, GLA_DK, GLA_DV), F32)],
        scratch_shapes=[pltpu.VMEM((GLA_HEADS, GLA_DK, GLA_DV), F32),
                        pltpu.VMEM((GMLP_GROUPS, GMLP_CHUNK, GMLP_CHUNK), BF16),
                        pltpu.VMEM((2, tile, D_MODEL), BF16),
                        pltpu.VMEM((tile, D_MODEL), F32)],
        compiler_params=pltpu.CompilerParams(dimension_semantics=("arbitrary",),
                                             vmem_limit_bytes=VMEM_LIMIT_BYTES),
        name="prompt_layer",
    )(x, x, ada, *small, *weights)


def _sample_kernel(x_ref, ada_ref, s0_ref, norm_w_ref, b_a_ref, gla_nw_ref, lnw_ref, lnb_ref,
                   ws_ref, bs_ref, bgate_ref, final_w_ref, w_main_ref, w_alr_ref, w_a2_ref,
                   w_pa_ref, w_pb_ref, w_out_ref, y_ref, state_ref, vn_ref):
    nb, ns, _ = x_ref.shape
    rows = nb * ns
    stack = GLA_CHUNK // ns
    x3 = x_ref[...]
    shift = ada_ref[:, :, 0:D_MODEL]
    scale = ada_ref[:, :, D_MODEL:2 * D_MODEL]
    gate = ada_ref[:, :, 2 * D_MODEL:3 * D_MODEL]
    h = _modulated_input(x3, shift, scale, norm_w_ref[...]).reshape(rows, D_MODEL)

    blocks = [slice(g * GLA_CHUNK, (g + 1) * GLA_CHUNK) for g in range(rows // GLA_CHUNK)]
    r_i = lax.broadcasted_iota(jnp.int32, (GLA_CHUNK, GLA_CHUNK), 0)
    c_i = lax.broadcasted_iota(jnp.int32, (GLA_CHUNK, GLA_CHUNK), 1)
    mask = (r_i // ns == c_i // ns) & (r_i >= c_i)
    mask_bf16 = mask.astype(BF16)

    la = _log_decay(h, w_alr_ref, w_a2_ref, b_a_ref[...])
    q = _proj(h, w_main_ref, _OFF_Q, GLA_KEY)
    b = [_cumsum_rows(la[blk], mask_bf16) for blk in blocks]
    k = _proj(h, w_main_ref, _OFF_K, GLA_KEY)
    v = _proj(h, w_main_ref, _OFF_V, GLA_VAL).astype(BF16)
    ops = [_gla_decay_operands(q[blk], k[blk], b[g], ns) for g, blk in enumerate(blocks)]
    r = _proj(h, w_main_ref, _OFF_R, GLA_VAL)
    scores = [_gla_scores(ops[g][0], ops[g][1], mask) for g in range(len(blocks))]
    vn = _layernorm(_proj(h, w_main_ref, _OFF_GV, GMLP_WIDTH), lnw_ref[...], lnb_ref[...])
    vn_ref[...] = vn.reshape(nb, ns, GMLP_WIDTH)

    gated = []
    for g, blk in enumerate(blocks):
        qs, kl, e_last = ops[g][2], ops[g][3], ops[g][4]
        inter = [[] for _ in range(GLA_HEADS)]
        for s in range(stack):
            stream = g * stack + s
            seg = slice(s * ns, (s + 1) * ns)
            v_s = v[blk][seg]
            steps = _gla_state_step(kl[seg], v_s, e_last[s])
            for hd in range(GLA_HEADS):
                s0 = s0_ref[stream, hd]
                inter[hd].append(_mm(_head(qs[seg], hd, GLA_DK), s0))
                state_ref[stream, hd] = s0 * steps[hd][0] + steps[hd][1]
        o = [jnp.dot(scores[g][hd], _head(v[blk], hd, GLA_DV), preferred_element_type=F32)
             + jnp.concatenate(inter[hd], axis=0) for hd in range(GLA_HEADS)]
        gated.append(_head_rms_gate(o, r[blk], gla_nw_ref[...]))
    u = _proj(h, w_main_ref, _OFF_U, GMLP_WIDTH)

    ws = _masked_ws(ws_ref[...], ns).astype(BF16)
    s_gate = jnp.concatenate(
        [_spatial_gate(vn[st * ns:(st + 1) * ns], ws, bs_ref[...]) for st in range(nb)], axis=0)
    z = _proj(h, w_main_ref, _OFF_Z, GMLP_WIDTH)
    y_a = _mm(jnp.concatenate(gated, axis=0), w_pa_ref[...])
    g_a = _proj(h, w_main_ref, _OFF_GA, D_MODEL)
    y_b = _mm(u * s_gate * _silu(z), w_pb_ref[...])
    g_b = _proj(h, w_main_ref, _OFF_GB, D_MODEL)
    merged = (jax.nn.sigmoid(g_a + bgate_ref[0:1, :]) * y_a
              + jax.nn.sigmoid(g_b + bgate_ref[1:2, :]) * y_b)
    upd = _mm(merged, w_out_ref[...]).reshape(nb, ns, D_MODEL)
    y_ref[...] = _rms(x3 + gate * upd, final_w_ref[...])


def _sample_call(x, ada, s0, small, weights):
    nb, ns, _ = x.shape
    assert GLA_CHUNK % ns == 0 and (nb * ns) % GLA_CHUNK == 0
    args = (x, ada, s0, *small, *weights)
    full = lambda a: pl.BlockSpec(a.shape, lambda i, nd=a.ndim: (0,) * nd)
    return pl.pallas_call(
        _sample_kernel,
        grid=(1,),
        in_specs=[full(a) for a in args],
        out_specs=[pl.BlockSpec((nb, ns, D_MODEL), lambda i: (0, 0, 0)),
                   pl.BlockSpec(s0.shape, lambda i: (0, 0, 0, 0)),
                   pl.BlockSpec((nb, ns, GMLP_WIDTH), lambda i: (0, 0, 0))],
        out_shape=[jax.ShapeDtypeStruct((nb, ns, D_MODEL), F32),
                   jax.ShapeDtypeStruct(s0.shape, F32),
                   jax.ShapeDtypeStruct((nb, ns, GMLP_WIDTH), F32)],
        compiler_params=pltpu.CompilerParams(dimension_semantics=("arbitrary",),
                                             vmem_limit_bytes=VMEM_LIMIT_BYTES),
        name="sample_layer",
    )(*args)


def kernel(x_prompt, x_sample, state_gla, c_prompt, c_sample, norm_w, w_ada, b_ada, w_in, w_a2,
           b_a, gla_norm_w, ln_v_w, ln_v_b, w_s, b_s, b_gate, w_proj_a, w_proj_b, w_out,
           final_norm_w):
    depth = norm_w.shape[0]
    assert depth == 1 and x_prompt.shape[0] == 1
    nb, ns, _ = x_sample.shape
    lyr = 0

    c_all = jnp.concatenate([c_sample, c_prompt], axis=0)
    pad = (-c_all.shape[0]) % 8
    c_all = jnp.pad(c_all, ((0, pad), (0, 0)))
    ada = _ada_call(c_all, w_ada[lyr], b_ada[lyr][None, :])
    ada_sample = ada[:nb][:, None, :]
    ada_prompt = ada[nb:nb + 1]

    a0 = 2 * GLA_KEY + 2 * GLA_VAL
    w_t = jnp.transpose(w_in[lyr])
    w_main, w_alr = _repack_call(w_t, a0, GLA_LOWRANK)
    w_a2p = jnp.pad(w_a2[lyr], ((0, LANES - GLA_LOWRANK), (0, 0))).astype(BF16)
    weights = (w_main, w_alr, w_a2p, w_proj_a[lyr].astype(BF16), w_proj_b[lyr].astype(BF16),
               w_out[lyr].astype(BF16))

    def small(chunk):
        return (norm_w[lyr][None, :], b_a[lyr][None, :], gla_norm_w[lyr][None, :],
                ln_v_w[lyr][None, :], ln_v_b[lyr][None, :], w_s[lyr][:, :chunk, :chunk],
                b_s[lyr][:, :chunk].T, b_gate[lyr], final_norm_w[None, :])

    y_p, st_p = _prompt_call(x_prompt[0], ada_prompt, small(GMLP_CHUNK), weights)
    y_s, st_s, vn_s = _sample_call(x_sample, ada_sample, state_gla[lyr], small(min(GMLP_CHUNK, ns)),
                                   weights)
    return (y_p[None], y_s, st_p[None, None], st_s[None], vn_s[None])
```
